```python
import math
import jax, jax.numpy as jnp
from jax import lax
import numpy as np


D_MODEL = 1024
BATCH = 4
SEQ = 4096
DEPTH = 4
DEC_BATCH = 128
DEC_SEQ = 1
PAST_LEN = 2048
PAGE_SIZE = 128

N_A_LAYERS = DEPTH // 2
N_B_LAYERS = DEPTH - N_A_LAYERS
CHUNK = 128
GATE_DIM = 2 * D_MODEL
N_GROUPS = 8
GROUP_DIM = GATE_DIM // N_GROUPS
N_HEADS = 8
HEAD_DIM = D_MODEL // (2 * N_HEADS)
V_DIM = 2 * HEAD_DIM
ATTN_BLOCK = 128
MASK_VALUE = -1e9
PEER_HEADS = 8
N_KEYS = 128
N_EXPERTS = N_KEYS * N_KEYS
PEER_QDIM = 256
PEER_TOPK = 16
PEER_BLOCK = 128
EPS = 1e-6

kernel_name = 'yoco_gmlp_diffattn_peer_step'


def rmsnorm(x, g):
    xf = x.astype(jnp.float32)
    y = xf * lax.rsqrt(jnp.mean(xf * xf, axis=-1, keepdims=True) + EPS)
    return (y * g.astype(jnp.float32)).astype(x.dtype)


def alibi_slopes():
    return jnp.asarray(2.0 ** (-8.0 * np.arange(1, N_HEADS + 1, dtype=np.float32) / N_HEADS), jnp.float32)


def gmlp_project(h, w_in, vnorm_g):
    z = jax.nn.gelu(h @ w_in)
    u, v = jnp.split(z, 2, axis=-1)
    return u, rmsnorm(v, vnorm_g)


def gmlp_mix(u, v, w_s, b_s, w_out):
    B, S, _ = v.shape
    L = min(S, CHUNK)
    nc = S // L
    causal = jnp.tril(jnp.ones((L, L), dtype=bool))
    w = jnp.where(causal, w_s[:, :L, :L], 0.0)
    vc = v.reshape(B, nc, L, N_GROUPS, GROUP_DIM)
    mixed = jnp.einsum('gts,bcsgd->bctgd', w, vc) + b_s[:, :L].T[:, :, None]
    gated = u * mixed.reshape(B, S, GATE_DIM)
    return gated @ w_out


def shared_kv(x, kv_norm_g, w_k, w_v, k_norm_g):
    B, T, _ = x.shape
    h = rmsnorm(x, kv_norm_g)
    k = rmsnorm((h @ w_k).reshape(B, T, 2 * N_HEADS, HEAD_DIM), k_norm_g)
    v = (h @ w_v).reshape(B, T, N_HEADS, V_DIM)
    return k, v


def diff_attn_core(q, k, v, q_pos, k_pos, lam, slopes):
    s = jnp.einsum('bqmhd,bkmhd->bmhqk', q, k, preferred_element_type=jnp.float32) * (HEAD_DIM ** -0.5)
    dist = (q_pos[:, None] - k_pos[None, :]).astype(jnp.float32)
    bias = jnp.where(dist >= 0, -slopes[:, None, None] * dist, MASK_VALUE)
    p = jax.nn.softmax(s + bias, axis=-1)
    a = p[:, 0] - lam * p[:, 1]
    return jnp.einsum('bhqk,bkhd->bqhd', a.astype(v.dtype), v)


def diff_attn_layer(x, k, v, q_pos, k_pos, norm_g, w_q, q_norm_g, lam, lam_init, subln_g, w_o):
    B, T, _ = x.shape
    h = rmsnorm(x, norm_g)
    q = rmsnorm((h @ w_q).reshape(B, T, 2 * N_HEADS, HEAD_DIM), q_norm_g).reshape(B, T, 2, N_HEADS, HEAD_DIM)
    kk = k.reshape(k.shape[0], k.shape[1], 2, N_HEADS, HEAD_DIM)
    slopes = alibi_slopes()
    if T > ATTN_BLOCK and T % ATTN_BLOCK == 0:
        nb = T // ATTN_BLOCK
        qb = q.reshape(B, nb, ATTN_BLOCK, 2, N_HEADS, HEAD_DIM).swapaxes(0, 1)
        pb = q_pos.reshape(nb, ATTN_BLOCK)
        o = lax.map(lambda a: diff_attn_core(a[0], kk, v, a[1], k_pos, lam, slopes), (qb, pb))
        o = o.swapaxes(0, 1).reshape(B, T, N_HEADS, V_DIM)
    else:
        o = diff_attn_core(q, kk, v, q_pos, k_pos, lam, slopes)
    o = rmsnorm(o, subln_g) * (1.0 - lam_init)
    return o.reshape(B, T, N_HEADS * V_DIM) @ w_o


def peer_block(h, w_q, keys, u_tab, v_tab):
    T = h.shape[0]
    q = (h @ w_q).reshape(T, PEER_HEADS, 2, PEER_QDIM // 2)
    s = jnp.einsum('thcd,hcnd->thcn', q, keys, preferred_element_type=jnp.float32)
    sv, si = lax.top_k(s, PEER_TOPK)
    cand = sv[:, :, 0, :, None] + sv[:, :, 1, None, :]
    cidx = si[:, :, 0, :, None] * N_KEYS + si[:, :, 1, None, :]
    top_s, pos = lax.top_k(cand.reshape(T, PEER_HEADS, PEER_TOPK * PEER_TOPK), PEER_TOPK)
    eidx = jnp.take_along_axis(cidx.reshape(T, PEER_HEADS, PEER_TOPK * PEER_TOPK), pos, axis=-1)
    g = jax.nn.softmax(top_s, axis=-1)
    u = u_tab[eidx]
    vv = v_tab[eidx]
    act = jax.nn.gelu(jnp.einsum('thkd,td->thk', u, h, preferred_element_type=jnp.float32))
    return jnp.einsum('thk,thkd->td', (g * act).astype(h.dtype), vv)


def peer_ffn(h, w_q, keys, u_tab, v_tab):
    B, T, D = h.shape
    n = B * T
    nb = -(-n // PEER_BLOCK)
    flat = jnp.pad(h.reshape(n, D), ((0, nb * PEER_BLOCK - n), (0, 0)))
    out = lax.map(lambda blk: peer_block(blk, w_q, keys, u_tab, v_tab), flat.reshape(nb, PEER_BLOCK, D))
    return out.reshape(nb * PEER_BLOCK, D)[:n].reshape(B, T, D)


def setup_inputs(seed: int = 0) -> dict:
    key = jax.random.key(seed)
    ks = jax.random.split(key, 32)

    def nrm(k, shape, scale):
        return jax.random.normal(k, shape, jnp.float32) * scale

    def gain(k, shape):
        return 1.0 + 0.02 * jax.random.normal(k, shape, jnp.float32)

    n_pages = PAST_LEN // PAGE_SIZE
    n_used = DEC_BATCH * n_pages
    n_pool = n_used + max(1, n_used // 4)
    page_table = jax.random.permutation(ks[0], n_pool)[:n_used].reshape(DEC_BATCH, n_pages).astype(jnp.int32)
    return {
        'x_prompt': nrm(ks[1], (BATCH, SEQ, D_MODEL), 1.0),
        'x_sample': nrm(ks[2], (DEC_BATCH, DEC_SEQ, D_MODEL), 1.0),
        'cache_k': nrm(ks[3], (n_pool, PAGE_SIZE, 2 * N_HEADS, HEAD_DIM), 1.0),
        'cache_v': nrm(ks[4], (n_pool, PAGE_SIZE, N_HEADS, V_DIM), 1.0),
        'page_table': page_table,
        'a_norm_g': gain(ks[5], (N_A_LAYERS, D_MODEL)),
        'a_w_in': nrm(ks[6], (N_A_LAYERS, D_MODEL, 2 * GATE_DIM), D_MODEL ** -0.5),
        'a_vnorm_g': gain(ks[7], (N_A_LAYERS, GATE_DIM)),
        'a_w_s': nrm(ks[8], (N_A_LAYERS, N_GROUPS, CHUNK, CHUNK), CHUNK ** -0.5),
        'a_b_s': gain(ks[9], (N_A_LAYERS, N_GROUPS, CHUNK)),
        'a_w_out': nrm(ks[10], (N_A_LAYERS, GATE_DIM, D_MODEL), GATE_DIM ** -0.5),
        'kv_norm_g': gain(ks[11], (D_MODEL,)),
        'w_k': nrm(ks[12], (D_MODEL, 2 * N_HEADS * HEAD_DIM), D_MODEL ** -0.5),
        'w_v': nrm(ks[13], (D_MODEL, N_HEADS * V_DIM), D_MODEL ** -0.5),
        'k_norm_g': gain(ks[14], (HEAD_DIM,)),
        'b_norm_g': gain(ks[15], (N_B_LAYERS, D_MODEL)),
        'w_q': nrm(ks[16], (N_B_LAYERS, D_MODEL, 2 * N_HEADS * HEAD_DIM), D_MODEL ** -0.5),
        'q_norm_g': gain(ks[17], (N_B_LAYERS, HEAD_DIM)),
        'lambda_q1': nrm(ks[18], (N_B_LAYERS, HEAD_DIM), 0.1),
        'lambda_k1': nrm(ks[19], (N_B_LAYERS, HEAD_DIM), 0.1),
        'lambda_q2': nrm(ks[20], (N_B_LAYERS, HEAD_DIM), 0.1),
        'lambda_k2': nrm(ks[21], (N_B_LAYERS, HEAD_DIM), 0.1),
        'subln_g': gain(ks[22], (N_B_LAYERS, V_DIM)),
        'w_o': nrm(ks[23], (N_B_LAYERS, N_HEADS * V_DIM, D_MODEL), (N_HEADS * V_DIM) ** -0.5),
        'f_norm_g': gain(ks[24], (DEPTH, D_MODEL)),
        'peer_w_q': nrm(ks[25], (DEPTH, D_MODEL, PEER_HEADS * PEER_QDIM), D_MODEL ** -0.5),
        'peer_keys': nrm(ks[26], (DEPTH, PEER_HEADS, 2, N_KEYS, PEER_QDIM // 2), (PEER_QDIM // 2) ** -0.5),
        'peer_u': nrm(ks[27], (DEPTH, N_EXPERTS, D_MODEL), D_MODEL ** -0.5),
        'peer_v': nrm(ks[28], (DEPTH, N_EXPERTS, D_MODEL), (PEER_HEADS * PEER_TOPK) ** -0.5),
    }


def reference(x_prompt, x_sample, cache_k, cache_v, page_table,
              a_norm_g, a_w_in, a_vnorm_g, a_w_s, a_b_s, a_w_out,
              kv_norm_g, w_k, w_v, k_norm_g,
              b_norm_g, w_q, q_norm_g, lambda_q1, lambda_k1, lambda_q2, lambda_k2, subln_g, w_o,
              f_norm_g, peer_w_q, peer_keys, peer_u, peer_v):
    xp, xs = x_prompt, x_sample
    seq = xp.shape[1]
    dec_b, n_pages = page_table.shape
    past_len = n_pages * cache_k.shape[1]
    dec_seq = xs.shape[1]
    gmlp_v_rows = []
    for l in range(DEPTH):
        if l < N_A_LAYERS:
            i = l
            up, vp = gmlp_project(rmsnorm(xp, a_norm_g[i]), a_w_in[i], a_vnorm_g[i])
            xp = xp + gmlp_mix(up, vp, a_w_s[i], a_b_s[i], a_w_out[i])
            us, vs = gmlp_project(rmsnorm(xs, a_norm_g[i]), a_w_in[i], a_vnorm_g[i])
            xs = xs + gmlp_mix(us, vs, a_w_s[i], a_b_s[i], a_w_out[i])
            gmlp_v_rows.append(vs)
        else:
            if l == N_A_LAYERS:
                new_k_prompt, new_v_prompt = shared_kv(xp, kv_norm_g, w_k, w_v, k_norm_g)
                new_k_sample, new_v_sample = shared_kv(xs, kv_norm_g, w_k, w_v, k_norm_g)
                past_k = cache_k[page_table].reshape(dec_b, past_len, 2 * N_HEADS, HEAD_DIM)
                past_v = cache_v[page_table].reshape(dec_b, past_len, N_HEADS, V_DIM)
                k_s = jnp.concatenate([past_k, new_k_sample.astype(past_k.dtype)], axis=1)
                v_s = jnp.concatenate([past_v, new_v_sample.astype(past_v.dtype)], axis=1)
                pos_p = jnp.arange(seq, dtype=jnp.int32)
                qpos_s = past_len + jnp.arange(dec_seq, dtype=jnp.int32)
                kpos_s = jnp.arange(past_len + dec_seq, dtype=jnp.int32)
            i = l - N_A_LAYERS
            lam_init = 0.8 - 0.6 * math.exp(-0.3 * l)
            lam = (jnp.exp(jnp.sum(lambda_q1[i].astype(jnp.float32) * lambda_k1[i].astype(jnp.float32)))
                   - jnp.exp(jnp.sum(lambda_q2[i].astype(jnp.float32) * lambda_k2[i].astype(jnp.float32)))
                   + lam_init)
            xp = xp + diff_attn_layer(xp, new_k_prompt, new_v_prompt, pos_p, pos_p, b_norm_g[i], w_q[i],
                                      q_norm_g[i], lam, lam_init, subln_g[i], w_o[i])
            xs = xs + diff_attn_layer(xs, k_s, v_s, qpos_s, kpos_s, b_norm_g[i], w_q[i],
                                      q_norm_g[i], lam, lam_init, subln_g[i], w_o[i])
        xp = xp + peer_ffn(rmsnorm(xp, f_norm_g[l]), peer_w_q[l], peer_keys[l], peer_u[l], peer_v[l])
        xs = xs + peer_ffn(rmsnorm(xs, f_norm_g[l]), peer_w_q[l], peer_keys[l], peer_u[l], peer_v[l])
    state_gmlp_v = jnp.stack(gmlp_v_rows, axis=0)
    y_prompt = xp
    y_sample = xs
    return (y_prompt, y_sample, new_k_prompt, new_v_prompt, new_k_sample, new_v_sample, state_gmlp_v)
```

```python
import functools
import math

import numpy as np
import jax
import jax.numpy as jnp
from jax import lax
from jax.experimental import pallas as pl
from jax.experimental.pallas import tpu as pltpu

F32 = jnp.float32
BF = jnp.bfloat16

EPS = 1e-6
MASK_VALUE = -1e9
M_INIT = -1e30

LANES = 128
CHUNK = 128
N_GROUPS = 8
N_HEADS = 8
HEAD_DIM = 64
V_DIM = 128
PAGE = 128
PEER_HEADS = 8
N_KEYS = 128
PEER_TOPK = 16

VMEM_LIMIT = 56 * 1024 * 1024

TM_DENSE = 512
TM_PEER = 512
TE_PEER = 512
TQ_ATTN = 512
PAGES_PER_STEP = 4


def _params(sem):
    return pltpu.CompilerParams(dimension_semantics=sem, vmem_limit_bytes=VMEM_LIMIT)


def _gelu(x):
    return 0.5 * x * (1.0 + jnp.tanh(0.7978845608028654 * (x + 0.044715 * (x * x * x))))


def _rms(x, g):
    ms = jnp.mean(x * x, axis=-1, keepdims=True)
    return x * lax.rsqrt(ms + EPS) * g


def _mm(a, b):
    return jnp.dot(a, b, preferred_element_type=F32)


def _mm_nt(a, b):
    return lax.dot_general(a, b, (((1,), (1,)), ((), ())), preferred_element_type=F32)


def _row_spec(tm, d):
    return pl.BlockSpec((tm, d), lambda i: (i, 0))


def _full_spec(shape):
    nd = len(shape)
    return pl.BlockSpec(shape, lambda *_: (0,) * nd)


def _block_rows(n, pref):
    tm = min(pref, n)
    assert n % tm == 0, (n, tm)
    return tm


def _gmlp_front(x_ref, ng_ref, win_ref, vg_ref):
    x = x_ref[...]
    xn = _rms(x, ng_ref[...]).astype(BF)
    z = _gelu(_mm(xn, win_ref[...]))
    gd = z.shape[1] // 2
    return x, z[:, :gd], _rms(z[:, gd:], vg_ref[...])


def _gmlp_prompt_body(x_ref, ng_ref, win_ref, vg_ref, ws_ref, bias_ref, wout_ref, xo_ref, gated_ref):
    x, u, vn = _gmlp_front(x_ref, ng_ref, win_ref, vg_ref)
    vb = vn.astype(BF)
    tm, gd = u.shape
    gw = gd // N_GROUPS
    row = lax.broadcasted_iota(jnp.int32, (CHUNK, CHUNK), 0)
    col = lax.broadcasted_iota(jnp.int32, (CHUNK, CHUNK), 1)
    causal = col <= row
    for g in range(N_GROUPS):
        w = jnp.where(causal, ws_ref[g], 0.0).astype(BF)
        for c in range(tm // CHUNK):
            rs = slice(c * CHUNK, (c + 1) * CHUNK)
            cs = slice(g * gw, (g + 1) * gw)
            mixed = _mm(w, vb[rs, cs]) + bias_ref[:, cs]
            gated_ref[rs, cs] = (u[rs, cs] * mixed).astype(BF)
    xo_ref[...] = x + _mm(gated_ref[...], wout_ref[...])


def _gmlp_sample_body(x_ref, ng_ref, win_ref, vg_ref, w0_ref, b0_ref, wout_ref, xo_ref, vn_ref):
    x, u, vn = _gmlp_front(x_ref, ng_ref, win_ref, vg_ref)
    vn_ref[...] = vn
    gated = (u * (vn * w0_ref[...] + b0_ref[...])).astype(BF)
    xo_ref[...] = x + _mm(gated, wout_ref[...])


def _gmlp_prompt(x, ng, win, vg, ws, bs, wout):
    n, d = x.shape
    gd = wout.shape[0]
    tm = _block_rows(n, TM_DENSE)
    assert tm % CHUNK == 0
    bias = jnp.repeat(bs.T, gd // N_GROUPS, axis=1)
    return pl.pallas_call(
        _gmlp_prompt_body,
        out_shape=jax.ShapeDtypeStruct((n, d), F32),
        grid=(n // tm,),
        in_specs=[_row_spec(tm, d), _full_spec((1, d)), _full_spec(win.shape), _full_spec((1, gd)),
                  _full_spec(ws.shape), _full_spec(bias.shape), _full_spec(wout.shape)],
        out_specs=_row_spec(tm, d),
        scratch_shapes=[pltpu.VMEM((tm, gd), BF)],
        compiler_params=_params(("parallel",)),
    )(x, ng.reshape(1, d), win, vg.reshape(1, gd), ws, bias, wout)


def _gmlp_sample(x, ng, win, vg, ws, bs, wout):
    n, d = x.shape
    gd = wout.shape[0]
    gw = gd // N_GROUPS
    tm = _block_rows(n, TM_DENSE)
    w0 = jnp.repeat(ws[:, 0, 0], gw).reshape(1, gd)
    b0 = jnp.repeat(bs[:, 0], gw).reshape(1, gd)
    return pl.pallas_call(
        _gmlp_sample_body,
        out_shape=(jax.ShapeDtypeStruct((n, d), F32), jax.ShapeDtypeStruct((n, gd), F32)),
        grid=(n // tm,),
        in_specs=[_row_spec(tm, d), _full_spec((1, d)), _full_spec(win.shape), _full_spec((1, gd)),
                  _full_spec((1, gd)), _full_spec((1, gd)), _full_spec(wout.shape)],
        out_specs=(_row_spec(tm, d), _row_spec(tm, gd)),
        compiler_params=_params(("parallel",)),
    )(x, ng.reshape(1, d), win, vg.reshape(1, gd), w0, b0, wout)


def _head_norm(y, seg_ref, gt):
    y2 = y * y
    hi = y2.astype(BF)
    lo = (y2 - hi.astype(F32)).astype(BF)
    ms = _mm(hi, seg_ref[...]) + _mm(lo, seg_ref[...])
    return y * lax.rsqrt(ms + EPS) * gt


def _seg_mean_matrix(d, seg):
    idx = np.arange(d) // seg
    return jnp.asarray((idx[:, None] == idx[None, :]).astype(np.float32) / seg, BF)


def _kv_body(x_ref, g_ref, wk_ref, wv_ref, seg_ref, kg_ref, k_ref, v_ref, kb_ref, vb_ref):
    h = _rms(x_ref[...], g_ref[...]).astype(BF)
    k = _head_norm(_mm(h, wk_ref[...]), seg_ref, kg_ref[...])
    v = _mm(h, wv_ref[...])
    k_ref[...] = k
    v_ref[...] = v
    kb_ref[...] = k.astype(BF)
    vb_ref[...] = v.astype(BF)


def _shared_kv(x, g, wk, wv, seg, kg):
    n, d = x.shape
    dk, dv = wk.shape[1], wv.shape[1]
    tm = _block_rows(n, TM_DENSE)
    return pl.pallas_call(
        _kv_body,
        out_shape=(jax.ShapeDtypeStruct((n, dk), F32), jax.ShapeDtypeStruct((n, dv), F32),
                   jax.ShapeDtypeStruct((n, dk), BF), jax.ShapeDtypeStruct((n, dv), BF)),
        grid=(n // tm,),
        in_specs=[_row_spec(tm, d), _full_spec((1, d)), _full_spec(wk.shape), _full_spec(wv.shape),
                  _full_spec(seg.shape), _full_spec((1, dk))],
        out_specs=(_row_spec(tm, dk), _row_spec(tm, dv), _row_spec(tm, dk), _row_spec(tm, dv)),
        compiler_params=_params(("parallel",)),
    )(x, g.reshape(1, d), wk, wv, seg, kg)


def _q_body(x_ref, g_ref, wq_ref, seg_ref, qg_ref, q_ref):
    h = _rms(x_ref[...], g_ref[...]).astype(BF)
    q = _head_norm(_mm(h, wq_ref[...]), seg_ref, qg_ref[...])
    q_ref[...] = (q * (HEAD_DIM ** -0.5)).astype(BF)


def _q_proj(x, g, wq, seg, qg):
    n, d = x.shape
    dq = wq.shape[1]
    tm = _block_rows(n, TM_DENSE)
    return pl.pallas_call(
        _q_body,
        out_shape=jax.ShapeDtypeStruct((n, dq), BF),
        grid=(n // tm,),
        in_specs=[_row_spec(tm, d), _full_spec((1, d)), _full_spec(wq.shape), _full_spec(seg.shape),
                  _full_spec((1, dq))],
        out_specs=_row_spec(tm, dq),
        compiler_params=_params(("parallel",)),
    )(x, g.reshape(1, d), wq, seg, qg)


def _oproj_body(x_ref, o_ref, w_ref, xo_ref):
    xo_ref[...] = x_ref[...] + _mm(o_ref[...], w_ref[...])


def _o_proj(x, o, w):
    n, d = x.shape
    tm = _block_rows(n, TM_DENSE)
    return pl.pallas_call(
        _oproj_body,
        out_shape=jax.ShapeDtypeStruct((n, d), F32),
        grid=(n // tm,),
        in_specs=[_row_spec(tm, d), _row_spec(tm, o.shape[1]), _full_spec(w.shape)],
        out_specs=_row_spec(tm, d),
        compiler_params=_params(("parallel",)),
    )(x, o, w)


def _lambda(lq1, lk1, lq2, lk2, lam_init):
    a = jnp.sum(lq1[...] * lk1[...], axis=-1, keepdims=True)
    b = jnp.sum(lq2[...] * lk2[...], axis=-1, keepdims=True)
    return jnp.exp(a) - jnp.exp(b) + lam_init


def _sub_norm(o, sg, lam_init):
    ms = jnp.mean(o * o, axis=-1, keepdims=True)
    return o * lax.rsqrt(ms + EPS) * sg * (1.0 - lam_init)


def _online_softmax_step(s, v, m_ref, l_ref, acc_ref, idx):
    m_old = m_ref[idx]
    m_new = jnp.maximum(m_old, jnp.max(s, axis=1, keepdims=True))
    alpha = jnp.exp(m_old - m_new)
    p = jnp.exp(s - m_new[:, :1])
    l_ref[idx] = alpha * l_ref[idx] + jnp.sum(p, axis=1, keepdims=True)
    acc_ref[idx] = alpha[:, :1] * acc_ref[idx] + _mm(p.astype(BF), v)
    m_ref[idx] = m_new


def _flash_body(q_ref, k_ref, v_ref, lq1, lk1, lq2, lk2, sg_ref, o_ref, m_ref, l_ref, acc_ref,
                *, lam_init, slopes, tq):
    qi = pl.program_id(1)
    kj = pl.program_id(2)

    @pl.when(kj == 0)
    def _():
        m_ref[...] = jnp.full(m_ref.shape, M_INIT, F32)
        l_ref[...] = jnp.zeros(l_ref.shape, F32)
        acc_ref[...] = jnp.zeros(acc_ref.shape, F32)

    @pl.when(kj <= qi)
    def _():
        row = lax.broadcasted_iota(jnp.int32, (tq, tq), 0)
        col = lax.broadcasted_iota(jnp.int32, (tq, tq), 1)
        ndist = (col - row + (kj - qi) * tq).astype(F32)
        allowed = ndist <= 0.0
        for h in range(N_HEADS):
            bias = jnp.where(allowed, slopes[h] * ndist, MASK_VALUE)
            vh = v_ref[:, h * V_DIM:(h + 1) * V_DIM]
            for m in range(2):
                mh = m * N_HEADS + h
                s = _mm_nt(q_ref[mh], k_ref[mh]) + bias
                _online_softmax_step(s, vh, m_ref, l_ref, acc_ref, mh)

    @pl.when(kj == qi)
    def _():
        lam = _lambda(lq1, lk1, lq2, lk2, lam_init)
        for h in range(N_HEADS):
            o0 = acc_ref[h] / l_ref[h]
            o1 = acc_ref[N_HEADS + h] / l_ref[N_HEADS + h]
            o = _sub_norm(o0 - lam * o1, sg_ref[...], lam_init)
            o_ref[:, h * V_DIM:(h + 1) * V_DIM] = o.astype(BF)


def _prompt_attention(q_hm, k_hm, v, lams, sg, lam_init, slopes):
    b, nh2, t, hd = q_hm.shape
    dv = v.shape[2]
    tq = _block_rows(t, TQ_ATTN)
    nq = t // tq
    lam_specs = [_full_spec((1, hd))] * 4
    body = functools.partial(_flash_body, lam_init=lam_init, slopes=slopes, tq=tq)
    return pl.pallas_call(
        body,
        out_shape=jax.ShapeDtypeStruct((b, t, dv), BF),
        grid=(b, nq, nq),
        in_specs=[pl.BlockSpec((None, nh2, tq, hd), lambda bi, qi, kj: (bi, 0, qi, 0)),
                  pl.BlockSpec((None, nh2, tq, hd), lambda bi, qi, kj: (bi, 0, jnp.minimum(kj, qi), 0)),
                  pl.BlockSpec((None, tq, dv), lambda bi, qi, kj: (bi, jnp.minimum(kj, qi), 0)),
                  *lam_specs, _full_spec((1, V_DIM))],
        out_specs=pl.BlockSpec((None, tq, dv), lambda bi, qi, kj: (bi, qi, 0)),
        scratch_shapes=[pltpu.VMEM((nh2, tq, LANES), F32), pltpu.VMEM((nh2, tq, LANES), F32),
                        pltpu.VMEM((nh2, tq, V_DIM), F32)],
        compiler_params=_params(("parallel", "parallel", "arbitrary")),
    )(q_hm, k_hm, v, *lams, sg)


def _decode_body(pt_ref, q_ref, kn_ref, vn_ref, slope_ref, *rest, lam_init, past_len, n_pages_step):
    kp_refs = rest[:n_pages_step]
    vp_refs = rest[n_pages_step:2 * n_pages_step]
    lq1, lk1, lq2, lk2, sg_ref, o_ref, qrows_ref, m_ref, l_ref, acc_ref = rest[2 * n_pages_step:]
    g = pl.program_id(1)
    nh2, d = qrows_ref.shape

    @pl.when(g == 0)
    def _():
        q = q_ref[...].astype(F32)
        row = lax.broadcasted_iota(jnp.int32, (nh2, d), 0)
        lane = lax.broadcasted_iota(jnp.int32, (nh2, d), 1)
        qrows = jnp.where(lane // HEAD_DIM == row, q, 0.0)
        qrows_ref[...] = qrows.astype(BF)
        k_self = kn_ref[...].astype(BF).astype(F32)
        s_self = jnp.sum(qrows * k_self, axis=1, keepdims=True)
        m_ref[...] = jnp.broadcast_to(s_self, m_ref.shape)
        l_ref[...] = jnp.ones(l_ref.shape, F32)
        acc_ref[...] = jnp.broadcast_to(vn_ref[...].astype(BF).astype(F32), acc_ref.shape)

    nk = n_pages_step * PAGE
    kp = jnp.concatenate([r[...].astype(BF) for r in kp_refs], axis=0)
    vp = jnp.concatenate([r[...].astype(BF) for r in vp_refs], axis=0)
    kpos = lax.broadcasted_iota(jnp.int32, (nh2, nk), 1) + g * nk
    bias = slope_ref[:, :1] * (kpos - past_len).astype(F32)
    s = _mm_nt(qrows_ref[...], kp) + bias
    m_old = m_ref[...]
    m_new = jnp.maximum(m_old, jnp.max(s, axis=1, keepdims=True))
    alpha = jnp.exp(m_old - m_new)
    p = jnp.exp(s - m_new[:, :1])
    l_ref[...] = alpha * l_ref[...] + jnp.sum(p, axis=1, keepdims=True)
    acc_ref[...] = alpha[:, :1] * acc_ref[...] + _mm(p.astype(BF), vp)
    m_ref[...] = m_new

    @pl.when(g == pl.num_programs(1) - 1)
    def _():
        lam = _lambda(lq1, lk1, lq2, lk2, lam_init)
        for h in range(N_HEADS):
            cs = slice(h * V_DIM, (h + 1) * V_DIM)
            o0 = acc_ref[h:h + 1, cs] / l_ref[h:h + 1, :]
            o1 = acc_ref[N_HEADS + h:N_HEADS + h + 1, cs] / l_ref[N_HEADS + h:N_HEADS + h + 1, :]
            o = _sub_norm(o0 - lam * o1, sg_ref[...], lam_init)
            o_ref[:, cs] = o.astype(BF)


def _decode_attention(q, k_new, v_new, cache_k, cache_v, page_table, lams, sg, lam_init, slopes):
    r, d = q.shape
    n_pool = cache_k.shape[0]
    n_pages = page_table.shape[1]
    pps = min(PAGES_PER_STEP, n_pages)
    assert n_pages % pps == 0 and cache_k.shape[1] == PAGE
    ck = cache_k.reshape(n_pool, PAGE, d)
    cv = cache_v.reshape(n_pool, PAGE, d)
    nh2 = 2 * N_HEADS
    slope_arr = jnp.asarray(np.tile(np.asarray(slopes, np.float32), 2)[:, None] * np.ones((1, LANES), np.float32))

    def page_spec(jj):
        return pl.BlockSpec((None, PAGE, d), lambda ri, gi, pt: (pt[ri, gi * pps + jj], 0, 0))

    tok_spec = pl.BlockSpec((None, 1, d), lambda ri, gi, pt: (ri, 0, 0))

    def const_spec(shape):
        return pl.BlockSpec(shape, lambda ri, gi, pt: (0,) * len(shape))

    body = functools.partial(_decode_body, lam_init=lam_init, past_len=n_pages * PAGE, n_pages_step=pps)
    grid_spec = pltpu.PrefetchScalarGridSpec(
        num_scalar_prefetch=1,
        grid=(r, n_pages // pps),
        in_specs=[tok_spec, tok_spec, tok_spec, const_spec((nh2, LANES)),
                  *[page_spec(jj) for jj in range(pps)], *[page_spec(jj) for jj in range(pps)],
                  *[const_spec((1, HEAD_DIM))] * 4, const_spec((1, V_DIM))],
        out_specs=tok_spec,
        scratch_shapes=[pltpu.VMEM((nh2, d), BF), pltpu.VMEM((nh2, LANES), F32),
                        pltpu.VMEM((nh2, LANES), F32), pltpu.VMEM((nh2, d), F32)],
    )
    out = pl.pallas_call(
        body,
        out_shape=jax.ShapeDtypeStruct((r, 1, d), BF),
        grid_spec=grid_spec,
        compiler_params=_params(("parallel", "arbitrary")),
    )(page_table, q.reshape(r, 1, d), k_new.reshape(r, 1, d), v_new.reshape(r, 1, d), slope_arr,
      *([ck] * pps), *([cv] * pps), *lams, sg)
    return out.reshape(r, d)


def _top_rank(s, k):
    rows = lax.broadcasted_iota(jnp.int32, s.shape, 0).astype(F32)
    n_rows = float(s.shape[0])
    work = s
    rank = jnp.full(s.shape, float(k), F32)
    vals = []
    for a in range(k):
        m = jnp.max(work, axis=0, keepdims=True)
        first = jnp.min(jnp.where(work == m, rows, n_rows), axis=0, keepdims=True)
        hit = rows == first
        rank = jnp.where(hit, float(a), rank)
        work = jnp.where(hit, -jnp.inf, work)
        vals.append(m)
    return vals, rank


def _peer_select(h, lane0, keys_ref, qt_ref, a_ref, n_ref, e2_ref, r2_ref):
    k = PEER_TOPK
    lanes = pl.ds(lane0, LANES)

    def scores(c):
        hc = 2 * h + c
        qq = qt_ref[pl.ds(pl.multiple_of(hc * N_KEYS, N_KEYS), N_KEYS), lanes]
        return _mm(keys_ref[hc], qq)

    s1, s2 = scores(0), scores(1)
    v1, r1 = _top_rank(s1, k)
    v2, r2 = _top_rank(s2, k)
    v2_all = jnp.concatenate(v2, axis=0)
    cand = jnp.concatenate([v1[a] + v2_all for a in range(k)], axis=0)
    vc, rc = _top_rank(cand, k)
    chosen = jnp.where(rc < float(k), 1.0, 0.0)
    z = jnp.ones_like(vc[0])
    for t in range(1, k):
        z = z + jnp.exp(vc[t] - vc[0])
    n = jnp.zeros_like(s1)
    for a in range(k):
        count_a = jnp.sum(chosen[a * k:(a + 1) * k], axis=0, keepdims=True)
        n = n + jnp.where(r1 == float(a), count_a, 0.0)
    a_ref[h, :, lanes] = jnp.exp(s1 - v1[0]) / z
    n_ref[h, :, lanes] = n
    e2_ref[h, :, lanes] = jnp.exp(s2 - v2[0])
    r2_ref[h, :, lanes] = r2


def _peer_body(x_ref, g_ref, wqt_ref, keys_ref, u_ref, vt_ref, o_ref,
               ht_ref, qt_ref, acc_ref, gate_ref, a_ref, n_ref, e2_ref, r2_ref):
    j = pl.program_id(1)
    tm = x_ref.shape[0]
    te = u_ref.shape[0]
    nts = tm // LANES

    @pl.when(j == 0)
    def _():
        hn = _rms(x_ref[...], g_ref[...])
        ht = hn.T.astype(BF)
        ht_ref[...] = ht
        qt_ref[...] = _mm(wqt_ref[...], ht).astype(BF)
        acc_ref[...] = jnp.zeros(acc_ref.shape, F32)

        def select(it, carry):
            _peer_select(it // nts, pl.multiple_of((it % nts) * LANES, LANES),
                         keys_ref, qt_ref, a_ref, n_ref, e2_ref, r2_ref)
            return carry

        lax.fori_loop(0, PEER_HEADS * nts, select, 0)

    act = _gelu(_mm(u_ref[...], ht_ref[...]))
    for ii in range(te // N_KEYS):
        i = j * (te // N_KEYS) + ii
        w = jnp.zeros((N_KEYS, tm), F32)
        for h in range(PEER_HEADS):
            a_row = a_ref[h, pl.ds(i, 1), :]
            n_row = n_ref[h, pl.ds(i, 1), :]
            w = w + jnp.where(r2_ref[h] < n_row, e2_ref[h] * a_row, 0.0)
        rs = slice(ii * N_KEYS, (ii + 1) * N_KEYS)
        gate_ref[rs, :] = (act[rs, :] * w).astype(BF)
    acc_ref[...] += _mm(vt_ref[...], gate_ref[...])

    @pl.when(j == pl.num_programs(1) - 1)
    def _():
        o_ref[...] = x_ref[...] + acc_ref[...].T


def _peer(x, g, wqt, keys, u, vt):
    n, d = x.shape
    n_exp = u.shape[0]
    tm = _block_rows(n, TM_PEER)
    te = TE_PEER
    assert tm % LANES == 0 and n_exp % te == 0 and te % N_KEYS == 0
    table = pltpu.VMEM((PEER_HEADS, N_KEYS, tm), F32)
    return pl.pallas_call(
        _peer_body,
        out_shape=jax.ShapeDtypeStruct((n, d), F32),
        grid=(n // tm, n_exp // te),
        in_specs=[pl.BlockSpec((tm, d), lambda i, j: (i, 0)),
                  pl.BlockSpec((1, d), lambda i, j: (0, 0)),
                  pl.BlockSpec(wqt.shape, lambda i, j: (0, 0)),
                  pl.BlockSpec(keys.shape, lambda i, j: (0, 0, 0)),
                  pl.BlockSpec((te, d), lambda i, j: (j, 0)),
                  pl.BlockSpec((d, te), lambda i, j: (0, j))],
        out_specs=pl.BlockSpec((tm, d), lambda i, j: (i, 0)),
        scratch_shapes=[pltpu.VMEM((d, tm), BF), pltpu.VMEM((wqt.shape[0], tm), BF),
                        pltpu.VMEM((d, tm), F32), pltpu.VMEM((te, tm), BF),
                        table, table, table, table],
        compiler_params=_params(("parallel", "arbitrary")),
    )(x, g.reshape(1, d), wqt, keys, u, vt)


def kernel(x_prompt, x_sample, cache_k, cache_v, page_table, a_norm_g, a_w_in, a_vnorm_g, a_w_s, a_b_s, a_w_out,
           kv_norm_g, w_k, w_v, k_norm_g, b_norm_g, w_q, q_norm_g, lambda_q1, lambda_k1, lambda_q2, lambda_k2,
           subln_g, w_o, f_norm_g, peer_w_q, peer_keys, peer_u, peer_v):
    bsz, seq, d = x_prompt.shape
    r, dec_seq, _ = x_sample.shape
    assert dec_seq == 1 and seq % CHUNK == 0
    depth = f_norm_g.shape[0]
    n_a = a_norm_g.shape[0]
    nh2 = 2 * N_HEADS
    slopes = tuple(float(v) for v in 2.0 ** (-8.0 * np.arange(1, N_HEADS + 1, dtype=np.float32) / N_HEADS))

    xp = x_prompt.reshape(bsz * seq, d)
    xs = x_sample.reshape(r, d)
    seg = _seg_mean_matrix(nh2 * HEAD_DIM, HEAD_DIM)
    gmlp_v_rows = []
    for l in range(depth):
        if l < n_a:
            win = a_w_in[l].astype(BF)
            wout = a_w_out[l].astype(BF)
            xp = _gmlp_prompt(xp, a_norm_g[l], win, a_vnorm_g[l], a_w_s[l], a_b_s[l], wout)
            xs, vs = _gmlp_sample(xs, a_norm_g[l], win, a_vnorm_g[l], a_w_s[l], a_b_s[l], wout)
            gmlp_v_rows.append(vs.reshape(r, dec_seq, -1))
        else:
            if l == n_a:
                wk = w_k.astype(BF)
                wv = w_v.astype(BF)
                kg = jnp.tile(k_norm_g, nh2).reshape(1, nh2 * HEAD_DIM)
                kp, vp, kp_b, vp_b = _shared_kv(xp, kv_norm_g, wk, wv, seg, kg)
                ks, vs_new, _, _ = _shared_kv(xs, kv_norm_g, wk, wv, seg, kg)
                k_hm = kp_b.reshape(bsz, seq, nh2, HEAD_DIM).transpose(0, 2, 1, 3)
                v_b = vp_b.reshape(bsz, seq, N_HEADS * V_DIM)
            i = l - n_a
            lam_init = 0.8 - 0.6 * math.exp(-0.3 * l)
            wq = w_q[i].astype(BF)
            qg = jnp.tile(q_norm_g[i], nh2).reshape(1, nh2 * HEAD_DIM)
            lams = [v[i].reshape(1, HEAD_DIM) for v in (lambda_q1, lambda_k1, lambda_q2, lambda_k2)]
            sg = subln_g[i].reshape(1, V_DIM)
            wo = w_o[i].astype(BF)
            qp = _q_proj(xp, b_norm_g[i], wq, seg, qg)
            q_hm = qp.reshape(bsz, seq, nh2, HEAD_DIM).transpose(0, 2, 1, 3)
            op = _prompt_attention(q_hm, k_hm, v_b, lams, sg, lam_init, slopes)
            xp = _o_proj(xp, op.reshape(bsz * seq, -1), wo)
            qs = _q_proj(xs, b_norm_g[i], wq, seg, qg)
            os_ = _decode_attention(qs, ks, vs_new, cache_k, cache_v, page_table, lams, sg, lam_init, slopes)
            xs = _o_proj(xs, os_, wo)
        wqt = peer_w_q[l].T.astype(BF)
        keys = peer_keys[l].reshape(PEER_HEADS * 2, N_KEYS, -1).astype(BF)
        u = peer_u[l].astype(BF)
        vt = peer_v[l].T.astype(BF)
        xp = _peer(xp, f_norm_g[l], wqt, keys, u, vt)
        xs = _peer(xs, f_norm_g[l], wqt, keys, u, vt)

    state_gmlp_v = jnp.stack(gmlp_v_rows, axis=0)
    return (xp.reshape(bsz, seq, d), xs.reshape(r, dec_seq, d),
            kp.reshape(bsz, seq, nh2, HEAD_DIM), vp.reshape(bsz, seq, N_HEADS, V_DIM),
            ks.reshape(r, dec_seq, nh2, HEAD_DIM), vs_new.reshape(r, dec_seq, N_HEADS, V_DIM),
            state_gmlp_v)
```

```python
import functools
import math

import numpy as np
import jax
import jax.numpy as jnp
from jax import lax
from jax.experimental import pallas as pl
from jax.experimental.pallas import tpu as pltpu

F32 = jnp.float32
BF = jnp.bfloat16

EPS = 1e-6
MASK_VALUE = -1e9
M_INIT = -1e30
LOG2E = math.log2(math.e)

LANES = 128
CHUNK = 128
N_GROUPS = 8
N_HEADS = 8
HEAD_DIM = 64
V_DIM = 128
PAGE = 128
PEER_HEADS = 8
N_KEYS = 128
PEER_TOPK = 16

VMEM_LIMIT = 56 * 1024 * 1024

TM_DENSE = 512
TM_PEER = 512
TE_PEER = 512
TQ_ATTN = 512
HEAD_MAPS_PER_PASS = 8
PAGES_PER_STEP = 4


def _params(sem):
    return pltpu.CompilerParams(dimension_semantics=sem, vmem_limit_bytes=VMEM_LIMIT)


def _gelu(x):
    return 0.5 * x * (1.0 + jnp.tanh(0.7978845608028654 * (x + 0.044715 * (x * x * x))))


def _rms(x, g):
    ms = jnp.mean(x * x, axis=-1, keepdims=True)
    return x * lax.rsqrt(ms + EPS) * g


def _mm(a, b):
    return jnp.dot(a, b, preferred_element_type=F32)


def _mm_nt(a, b):
    return lax.dot_general(a, b, (((1,), (1,)), ((), ())), preferred_element_type=F32)


def _row_spec(tm, d):
    return pl.BlockSpec((tm, d), lambda i: (i, 0))


def _full_spec(shape):
    nd = len(shape)
    return pl.BlockSpec(shape, lambda *_: (0,) * nd)


def _block_rows(n, pref):
    tm = min(pref, n)
    assert n % tm == 0, (n, tm)
    return tm


def _gmlp_front(x_ref, ng_ref, win_ref, vg_ref):
    x = x_ref[...]
    xn = _rms(x, ng_ref[...]).astype(BF)
    z = _gelu(_mm(xn, win_ref[...]))
    gd = z.shape[1] // 2
    return x, z[:, :gd], _rms(z[:, gd:], vg_ref[...])


def _gmlp_prompt_body(x_ref, ng_ref, win_ref, vg_ref, ws_ref, bias_ref, wout_ref, xo_ref, gated_ref):
    x, u, vn = _gmlp_front(x_ref, ng_ref, win_ref, vg_ref)
    vb = vn.astype(BF)
    tm, gd = u.shape
    gw = gd // N_GROUPS
    row = lax.broadcasted_iota(jnp.int32, (CHUNK, CHUNK), 0)
    col = lax.broadcasted_iota(jnp.int32, (CHUNK, CHUNK), 1)
    causal = col <= row
    for g in range(N_GROUPS):
        w = jnp.where(causal, ws_ref[g], 0.0).astype(BF)
        for c in range(tm // CHUNK):
            rs = slice(c * CHUNK, (c + 1) * CHUNK)
            cs = slice(g * gw, (g + 1) * gw)
            mixed = _mm(w, vb[rs, cs]) + bias_ref[:, cs]
            gated_ref[rs, cs] = (u[rs, cs] * mixed).astype(BF)
    xo_ref[...] = x + _mm(gated_ref[...], wout_ref[...])


def _gmlp_sample_body(x_ref, ng_ref, win_ref, vg_ref, w0_ref, b0_ref, wout_ref, xo_ref, vn_ref):
    x, u, vn = _gmlp_front(x_ref, ng_ref, win_ref, vg_ref)
    vn_ref[...] = vn
    gated = (u * (vn * w0_ref[...] + b0_ref[...])).astype(BF)
    xo_ref[...] = x + _mm(gated, wout_ref[...])


def _gmlp_prompt(x, ng, win, vg, ws, bs, wout):
    n, d = x.shape
    gd = wout.shape[0]
    tm = _block_rows(n, TM_DENSE)
    assert tm % CHUNK == 0
    bias = jnp.repeat(bs.T, gd // N_GROUPS, axis=1)
    return pl.pallas_call(
        _gmlp_prompt_body,
        out_shape=jax.ShapeDtypeStruct((n, d), F32),
        grid=(n // tm,),
        in_specs=[_row_spec(tm, d), _full_spec((1, d)), _full_spec(win.shape), _full_spec((1, gd)),
                  _full_spec(ws.shape), _full_spec(bias.shape), _full_spec(wout.shape)],
        out_specs=_row_spec(tm, d),
        scratch_shapes=[pltpu.VMEM((tm, gd), BF)],
        compiler_params=_params(("parallel",)),
        name="gmlp_prompt",
    )(x, ng.reshape(1, d), win, vg.reshape(1, gd), ws, bias, wout)


def _gmlp_sample(x, ng, win, vg, ws, bs, wout):
    n, d = x.shape
    gd = wout.shape[0]
    gw = gd // N_GROUPS
    tm = _block_rows(n, TM_DENSE)
    w0 = jnp.repeat(ws[:, 0, 0], gw).reshape(1, gd)
    b0 = jnp.repeat(bs[:, 0], gw).reshape(1, gd)
    return pl.pallas_call(
        _gmlp_sample_body,
        out_shape=(jax.ShapeDtypeStruct((n, d), F32), jax.ShapeDtypeStruct((n, gd), F32)),
        grid=(n // tm,),
        in_specs=[_row_spec(tm, d), _full_spec((1, d)), _full_spec(win.shape), _full_spec((1, gd)),
                  _full_spec((1, gd)), _full_spec((1, gd)), _full_spec(wout.shape)],
        out_specs=(_row_spec(tm, d), _row_spec(tm, gd)),
        compiler_params=_params(("parallel",)),
        name="gmlp_sample",
    )(x, ng.reshape(1, d), win, vg.reshape(1, gd), w0, b0, wout)


def _head_norm(y, seg_ref, gt):
    y2 = y * y
    hi = y2.astype(BF)
    lo = (y2 - hi.astype(F32)).astype(BF)
    ms = _mm(hi, seg_ref[...]) + _mm(lo, seg_ref[...])
    return y * lax.rsqrt(ms + EPS) * gt


def _seg_mean_matrix(d, seg):
    idx = np.arange(d) // seg
    return jnp.asarray((idx[:, None] == idx[None, :]).astype(np.float32) / seg, BF)


def _kv_body(x_ref, g_ref, wk_ref, wv_ref, seg_ref, kg_ref, k_ref, v_ref, kb_ref, vb_ref):
    h = _rms(x_ref[...], g_ref[...]).astype(BF)
    k = _head_norm(_mm(h, wk_ref[...]), seg_ref, kg_ref[...])
    v = _mm(h, wv_ref[...])
    k_ref[...] = k
    v_ref[...] = v
    kb_ref[...] = k.astype(BF)
    vb_ref[...] = v.astype(BF)


def _shared_kv(x, g, wk, wv, seg, kg):
    n, d = x.shape
    dk, dv = wk.shape[1], wv.shape[1]
    tm = _block_rows(n, TM_DENSE)
    return pl.pallas_call(
        _kv_body,
        out_shape=(jax.ShapeDtypeStruct((n, dk), F32), jax.ShapeDtypeStruct((n, dv), F32),
                   jax.ShapeDtypeStruct((n, dk), BF), jax.ShapeDtypeStruct((n, dv), BF)),
        grid=(n // tm,),
        in_specs=[_row_spec(tm, d), _full_spec((1, d)), _full_spec(wk.shape), _full_spec(wv.shape),
                  _full_spec(seg.shape), _full_spec((1, dk))],
        out_specs=(_row_spec(tm, dk), _row_spec(tm, dv), _row_spec(tm, dk), _row_spec(tm, dv)),
        compiler_params=_params(("parallel",)),
        name="shared_kv",
    )(x, g.reshape(1, d), wk, wv, seg, kg)


def _q_body(x_ref, g_ref, wq_ref, seg_ref, qg_ref, q_ref):
    h = _rms(x_ref[...], g_ref[...]).astype(BF)
    q = _head_norm(_mm(h, wq_ref[...]), seg_ref, qg_ref[...])
    q_ref[...] = (q * (HEAD_DIM ** -0.5 * LOG2E)).astype(BF)


def _q_proj(x, g, wq, seg, qg):
    n, d = x.shape
    dq = wq.shape[1]
    tm = _block_rows(n, TM_DENSE)
    return pl.pallas_call(
        _q_body,
        out_shape=jax.ShapeDtypeStruct((n, dq), BF),
        grid=(n // tm,),
        in_specs=[_row_spec(tm, d), _full_spec((1, d)), _full_spec(wq.shape), _full_spec(seg.shape),
                  _full_spec((1, dq))],
        out_specs=_row_spec(tm, dq),
        compiler_params=_params(("parallel",)),
        name="q_proj",
    )(x, g.reshape(1, d), wq, seg, qg)


def _oproj_body(x_ref, o_ref, w_ref, xo_ref):
    xo_ref[...] = x_ref[...] + _mm(o_ref[...], w_ref[...])


def _o_proj(x, o, w):
    n, d = x.shape
    tm = _block_rows(n, TM_DENSE)
    return pl.pallas_call(
        _oproj_body,
        out_shape=jax.ShapeDtypeStruct((n, d), F32),
        grid=(n // tm,),
        in_specs=[_row_spec(tm, d), _row_spec(tm, o.shape[1]), _full_spec(w.shape)],
        out_specs=_row_spec(tm, d),
        compiler_params=_params(("parallel",)),
        name="o_proj",
    )(x, o, w)


def _lambda(lq1, lk1, lq2, lk2, lam_init):
    a = jnp.sum(lq1[...] * lk1[...], axis=-1, keepdims=True)
    b = jnp.sum(lq2[...] * lk2[...], axis=-1, keepdims=True)
    return jnp.exp(a) - jnp.exp(b) + lam_init


def _sub_norm(o, sg, lam_init):
    ms = jnp.mean(o * o, axis=-1, keepdims=True)
    return o * lax.rsqrt(ms + EPS) * sg * (1.0 - lam_init)


def _flash_body(tq_ref, tk_ref, q_ref, k_ref, v_ref, lq1, lk1, lq2, lk2, sg_ref, o_ref,
                m_ref, acc_ref, s_ref, p_ref, *, lam_init, slopes, tq):
    t = pl.program_id(1)
    qi = tq_ref[t]
    kj = tk_ref[t]
    nh2 = q_ref.shape[0]
    group = s_ref.shape[0]

    @pl.when(kj == 0)
    def _():
        m_ref[...] = jnp.full(m_ref.shape, M_INIT, F32)
        acc_ref[...] = jnp.zeros(acc_ref.shape, F32)

    def step(diagonal):
        kpos = (lax.broadcasted_iota(jnp.int32, (1, tq), 1) + (kj - qi) * tq).astype(F32)
        if diagonal:
            row = lax.broadcasted_iota(jnp.int32, (tq, tq), 0)
            col = lax.broadcasted_iota(jnp.int32, (tq, tq), 1)
            allowed = col <= row
        ones = jnp.ones((tq, V_DIM), BF)
        heads_per_pass = group // 2
        for h0 in range(0, N_HEADS, heads_per_pass):
            mhs = [m * N_HEADS + h for h in range(h0, h0 + heads_per_pass) for m in range(2)]
            for idx, mh in enumerate(mhs):
                s_ref[idx] = _mm_nt(q_ref[mh], k_ref[mh])
            for idx, mh in enumerate(mhs):
                s = s_ref[idx] + slopes[mh % N_HEADS] * kpos
                if diagonal:
                    s = jnp.where(allowed, s, MASK_VALUE)
                m_old = m_ref[mh]
                m_new = jnp.maximum(m_old, jnp.max(s, axis=1, keepdims=True))
                alpha = jnp.exp2(m_old - m_new)
                p_ref[idx] = jnp.exp2(s - jnp.tile(m_new, (1, tq // LANES))).astype(BF)
                m_ref[mh] = m_new
                acc_ref[mh] = jnp.tile(alpha, (1, 2)) * acc_ref[mh]
            for idx, mh in enumerate(mhs):
                h = mh % N_HEADS
                v_ext = jnp.concatenate([v_ref[:, h * V_DIM:(h + 1) * V_DIM], ones], axis=1)
                acc_ref[mh] += _mm(p_ref[idx], v_ext)

    @pl.when(kj < qi)
    def _():
        step(False)

    @pl.when(kj == qi)
    def _():
        step(True)
        lam = _lambda(lq1, lk1, lq2, lk2, lam_init)
        for h in range(N_HEADS):
            o0 = acc_ref[h, :, :V_DIM] / acc_ref[h, :, V_DIM:]
            o1 = acc_ref[N_HEADS + h, :, :V_DIM] / acc_ref[N_HEADS + h, :, V_DIM:]
            o = _sub_norm(o0 - lam * o1, sg_ref[...], lam_init)
            o_ref[:, h * V_DIM:(h + 1) * V_DIM] = o.astype(BF)


def _prompt_attention(q_hm, k_hm, v, lams, sg, lam_init, slopes):
    b, nh2, t, hd = q_hm.shape
    dv = v.shape[2]
    tq = _block_rows(t, TQ_ATTN)
    nq = t // tq
    tri_q = np.asarray([qi for qi in range(nq) for _ in range(qi + 1)], np.int32)
    tri_k = np.asarray([kj for qi in range(nq) for kj in range(qi + 1)], np.int32)

    def const_spec(shape):
        return pl.BlockSpec(shape, lambda bi, ti, tqr, tkr: (0,) * len(shape))

    body = functools.partial(_flash_body, lam_init=lam_init, slopes=slopes, tq=tq)
    grid_spec = pltpu.PrefetchScalarGridSpec(
        num_scalar_prefetch=2,
        grid=(b, len(tri_q)),
        in_specs=[pl.BlockSpec((None, nh2, tq, hd), lambda bi, ti, tqr, tkr: (bi, 0, tqr[ti], 0)),
                  pl.BlockSpec((None, nh2, tq, hd), lambda bi, ti, tqr, tkr: (bi, 0, tkr[ti], 0)),
                  pl.BlockSpec((None, tq, dv), lambda bi, ti, tqr, tkr: (bi, tkr[ti], 0)),
                  *[const_spec((1, hd))] * 4, const_spec((1, V_DIM))],
        out_specs=pl.BlockSpec((None, tq, dv), lambda bi, ti, tqr, tkr: (bi, tqr[ti], 0)),
        scratch_shapes=[pltpu.VMEM((nh2, tq, LANES), F32), pltpu.VMEM((nh2, tq, 2 * V_DIM), F32),
                        pltpu.VMEM((HEAD_MAPS_PER_PASS, tq, tq), F32),
                        pltpu.VMEM((HEAD_MAPS_PER_PASS, tq, tq), BF)],
    )
    return pl.pallas_call(
        body,
        out_shape=jax.ShapeDtypeStruct((b, t, dv), BF),
        grid_spec=grid_spec,
        compiler_params=_params(("parallel", "arbitrary")),
        name="prompt_attention",
    )(jnp.asarray(tri_q), jnp.asarray(tri_k), q_hm, k_hm, v, *lams, sg)


def _decode_body(pt_ref, q_ref, kn_ref, vn_ref, slope_ref, *rest, lam_init, past_len, pps):
    kp_refs = rest[:pps]
    vp_refs = rest[pps:2 * pps]
    lq1, lk1, lq2, lk2, sg_ref, o_ref, qt_ref, b0_ref, m_ref, l_ref, acc_ref = rest[2 * pps:]
    g = pl.program_id(1)
    nh2 = q_ref.shape[0]
    lane_sum = jnp.ones((HEAD_DIM, LANES), BF)

    @pl.when(g == 0)
    def _():
        q = q_ref[...].astype(F32)
        qt_ref[...] = jnp.broadcast_to(q[None], (PAGE, nh2, HEAD_DIM)).reshape(PAGE * nh2, HEAD_DIM)
        key = lax.broadcasted_iota(jnp.int32, (PAGE, nh2, LANES), 0).astype(F32)
        b0_ref[...] = slope_ref[...][None] * key
        m_ref[...] = _mm((q * kn_ref[...]).astype(BF), lane_sum)
        l_ref[...] = jnp.ones(l_ref.shape, F32)
        acc_ref[0] = vn_ref[...]
        acc_ref[1] = vn_ref[...]

    for jj in range(pps):
        k2 = kp_refs[jj][...].reshape(PAGE * nh2, HEAD_DIM)
        s = _mm((k2 * qt_ref[...]).astype(BF), lane_sum).reshape(PAGE, nh2, LANES) + b0_ref[...]
        c = slope_ref[...] * ((g * pps + jj) * PAGE - past_len).astype(F32)
        m_old = m_ref[...]
        m_new = jnp.maximum(m_old, jnp.max(s, axis=0) + c)
        p = jnp.exp2(s - (m_new - c)[None])
        alpha = jnp.exp2(m_old - m_new)
        l_ref[...] = alpha * l_ref[...] + jnp.sum(p, axis=0)
        m_ref[...] = m_new
        v = vp_refs[jj][...]
        for mm in range(2):
            hs = slice(mm * N_HEADS, (mm + 1) * N_HEADS)
            acc_ref[mm] = alpha[hs] * acc_ref[mm] + jnp.sum(p[:, hs, :] * v, axis=0)

    @pl.when(g == pl.num_programs(1) - 1)
    def _():
        lam = _lambda(lq1, lk1, lq2, lk2, lam_init)
        o0 = acc_ref[0] / l_ref[:N_HEADS, :]
        o1 = acc_ref[1] / l_ref[N_HEADS:, :]
        o_ref[...] = _sub_norm(o0 - lam * o1, sg_ref[...], lam_init).astype(BF)


def _decode_attention(q, k_new, v_new, cache_k, cache_v, page_table, lams, sg, lam_init, slopes):
    r, d = q.shape
    n_pages = page_table.shape[1]
    pps = min(PAGES_PER_STEP, n_pages)
    assert n_pages % pps == 0 and cache_k.shape[1] == PAGE
    nh2 = 2 * N_HEADS
    slope_arr = jnp.asarray(np.tile(np.asarray(slopes, np.float32), 2)[:, None] * np.ones((1, LANES), np.float32))

    def k_spec(jj):
        return pl.BlockSpec((None, PAGE, nh2, HEAD_DIM), lambda ri, gi, pt: (pt[ri, gi * pps + jj], 0, 0, 0))

    def v_spec(jj):
        return pl.BlockSpec((None, PAGE, N_HEADS, V_DIM), lambda ri, gi, pt: (pt[ri, gi * pps + jj], 0, 0, 0))

    def tok_spec(rows, cols):
        return pl.BlockSpec((None, rows, cols), lambda ri, gi, pt: (ri, 0, 0))

    def const_spec(shape):
        return pl.BlockSpec(shape, lambda ri, gi, pt: (0,) * len(shape))

    body = functools.partial(_decode_body, lam_init=lam_init, past_len=n_pages * PAGE, pps=pps)
    grid_spec = pltpu.PrefetchScalarGridSpec(
        num_scalar_prefetch=1,
        grid=(r, n_pages // pps),
        in_specs=[tok_spec(nh2, HEAD_DIM), tok_spec(nh2, HEAD_DIM), tok_spec(N_HEADS, V_DIM),
                  const_spec((nh2, LANES)),
                  *[k_spec(jj) for jj in range(pps)], *[v_spec(jj) for jj in range(pps)],
                  *[const_spec((1, HEAD_DIM))] * 4, const_spec((1, V_DIM))],
        out_specs=tok_spec(N_HEADS, V_DIM),
        scratch_shapes=[pltpu.VMEM((PAGE * nh2, HEAD_DIM), F32), pltpu.VMEM((PAGE, nh2, LANES), F32),
                        pltpu.VMEM((nh2, LANES), F32), pltpu.VMEM((nh2, LANES), F32),
                        pltpu.VMEM((2, N_HEADS, V_DIM), F32)],
    )
    out = pl.pallas_call(
        body,
        out_shape=jax.ShapeDtypeStruct((r, N_HEADS, V_DIM), BF),
        grid_spec=grid_spec,
        compiler_params=_params(("parallel", "arbitrary")),
        name="decode_attention",
    )(page_table, q.reshape(r, nh2, HEAD_DIM), k_new.reshape(r, nh2, HEAD_DIM),
      v_new.reshape(r, N_HEADS, V_DIM), slope_arr, *([cache_k] * pps), *([cache_v] * pps), *lams, sg)
    return out.reshape(r, d)


def _top_rank(s, k, order=None):
    if order is None:
        order = lax.broadcasted_iota(jnp.int32, s.shape, 0).astype(F32)
    work = s
    rank = jnp.full(s.shape, float(k), F32)
    vals = []
    for a in range(k):
        m = jnp.max(work, axis=0, keepdims=True)
        first = jnp.min(jnp.where(work == m, order, jnp.inf), axis=0, keepdims=True)
        hit = order == first
        rank = jnp.where(hit, float(a), rank)
        work = jnp.where(hit, -jnp.inf, work)
        vals.append(m)
    return vals, rank


def _top_values(s, k):
    work = s
    vals = []
    for _ in range(k):
        m = jnp.max(work, axis=0, keepdims=True)
        work = jnp.where(work == m, -jnp.inf, work)
        vals.append(m)
    taken = jnp.sum(jnp.where(work != s, 1.0, 0.0), axis=0, keepdims=True)
    return vals, taken == float(k)


KQ = 4


def _candidates(v1, v2, k):
    assert KQ * KQ >= k
    v1_all = jnp.concatenate(v1, axis=0)
    v2_all = jnp.concatenate(v2, axis=0)
    ar = lax.broadcasted_iota(jnp.int32, v1_all.shape, 0)
    low_a = jnp.where(ar >= KQ, v1_all, -jnp.inf)
    return jnp.concatenate([v1[a] + v2_all for a in range(KQ)] + [low_a + v2[b] for b in range(KQ)], axis=0)


def _peer_select(h, ts, keys_ref, qt_ref, a_ref, thr_ref, e2_ref, cmp_ref):
    k = PEER_TOPK
    lanes = pl.ds(pl.multiple_of(ts * LANES, LANES), LANES)

    def scores(c):
        hc = 2 * h + c
        qq = qt_ref[pl.ds(pl.multiple_of(hc * N_KEYS, N_KEYS), N_KEYS), lanes]
        return _mm(keys_ref[hc], qq)

    s1, s2 = scores(0), scores(1)

    v1, distinct1 = _top_values(s1, k)
    v2, distinct2 = _top_values(s2, k)
    cand = _candidates(v1, v2, k)
    vc, distinct_c = _top_values(cand, k)
    chosen = cand >= vc[k - 1]
    v2_all = jnp.concatenate(v2, axis=0)
    low = [jnp.min(jnp.where(chosen[a * k:(a + 1) * k], v2_all, jnp.inf), axis=0, keepdims=True)
           for a in range(KQ)]
    low_hi = jnp.where(chosen[KQ * k:(KQ + 1) * k], v2[0], jnp.inf)
    for b in range(1, KQ):
        low_hi = jnp.minimum(low_hi, jnp.where(chosen[(KQ + b) * k:(KQ + b + 1) * k], v2[b], jnp.inf))
    thr = jnp.full(s1.shape, jnp.inf, F32)
    for a in range(k):
        thr = jnp.where(s1 == v1[a], low[a] if a < KQ else low_hi[a:a + 1], thr)
    z = jnp.ones_like(vc[0])
    for t in range(1, k):
        z = z + jnp.exp(vc[t] - vc[0])
    a_ref[h, ts] = jnp.exp(s1 - v1[0]) / z
    thr_ref[h, ts] = thr
    e2_ref[h, ts] = jnp.exp(s2 - v2[0])
    cmp_ref[h, ts] = s2

    no_ties = jnp.min(jnp.where(distinct1 & distinct2 & distinct_c, 1.0, 0.0)) > 0.5

    @pl.when(jnp.logical_not(no_ties))
    def _():
        r1v, r1 = _top_rank(s1, k)
        r2v, r2 = _top_rank(s2, k)
        cand_r = _candidates(r1v, r2v, k)
        row = lax.broadcasted_iota(jnp.int32, cand_r.shape, 0)
        rb = row - KQ * k
        order = jnp.where(row < KQ * k, row, (rb % k) * k + rb // k).astype(F32)
        rcv, rc = _top_rank(cand_r, k, order)
        zr = jnp.ones_like(rcv[0])
        for t in range(1, k):
            zr = zr + jnp.exp(rcv[t] - rcv[0])
        a_ref[h, ts] = jnp.exp(s1 - r1v[0]) / zr
        picked = jnp.where(rc < float(k), 1.0, 0.0)
        count_hi = picked[KQ * k:(KQ + 1) * k]
        for b in range(1, KQ):
            count_hi = count_hi + picked[(KQ + b) * k:(KQ + b + 1) * k]
        n = jnp.zeros_like(s1)
        for a in range(k):
            if a < KQ:
                count_a = jnp.sum(picked[a * k:(a + 1) * k], axis=0, keepdims=True)
            else:
                count_a = count_hi[a:a + 1]
            n = n + jnp.where(r1 == float(a), count_a, 0.0)
        thr_ref[h, ts] = 0.5 - n
        cmp_ref[h, ts] = -r2


def _peer_body(x_ref, g_ref, wqt_ref, keys_ref, u_ref, vt_ref, o_ref,
               ht_ref, qt_ref, acc_ref, act0_ref, act1_ref, gate0_ref, gate1_ref, a_ref, thr_ref, e2_ref, cmp_ref):
    s = pl.program_id(1)
    n_blocks = pl.num_programs(1) - 2
    tm = x_ref.shape[0]
    te = u_ref.shape[0]
    nts = tm // LANES

    @pl.when(s == 0)
    def _():
        hn = _rms(x_ref[...], g_ref[...])
        ht = hn.T.astype(BF)
        ht_ref[...] = ht
        qt_ref[...] = _mm(wqt_ref[...], ht).astype(BF)
        acc_ref[...] = jnp.zeros(acc_ref.shape, F32)
        act1_ref[...] = jnp.zeros(act1_ref.shape, F32)
        gate0_ref[...] = jnp.zeros(gate0_ref.shape, BF)

        def select(it, carry):
            _peer_select(it // nts, it % nts, keys_ref, qt_ref, a_ref, thr_ref, e2_ref, cmp_ref)
            return carry

        lax.fori_loop(0, PEER_HEADS * nts, select, 0)

    gate_block = jnp.clip(s - 1, 0, n_blocks - 1)
    kdepth = 2 * N_KEYS

    def stages(act_cur, act_prev, gate_cur, gate_prev):
        for ii in range(te // N_KEYS):
            i = gate_block * (te // N_KEYS) + ii
            rs = slice(ii * N_KEYS, (ii + 1) * N_KEYS)
            act_cur[rs, :] = _mm(u_ref[rs, :], ht_ref[...])
            for lt in range(nts):
                ls = slice(lt * LANES, (lt + 1) * LANES)
                w = None
                for h in range(PEER_HEADS):
                    a_row = a_ref[h, lt, pl.ds(i, 1), :]
                    thr_row = thr_ref[h, lt, pl.ds(i, 1), :]
                    t = jnp.where(cmp_ref[h, lt] >= thr_row, e2_ref[h, lt] * a_row, 0.0)
                    w = t if w is None else w + t
                gate_cur[rs, ls] = (_gelu(act_prev[rs, ls]) * w).astype(BF)
            if (ii + 1) * N_KEYS % kdepth == 0:
                ks = slice((ii + 1) * N_KEYS - kdepth, (ii + 1) * N_KEYS)
                acc_ref[...] += _mm(vt_ref[:, ks], gate_prev[ks, :])

    @pl.when(s % 2 == 0)
    def _():
        stages(act0_ref, act1_ref, gate1_ref, gate0_ref)

    @pl.when(s % 2 == 1)
    def _():
        stages(act1_ref, act0_ref, gate0_ref, gate1_ref)

    @pl.when(s == pl.num_programs(1) - 1)
    def _():
        o_ref[...] = x_ref[...] + acc_ref[...].T


def _peer(x, g, wqt, keys, u, vt):
    n, d = x.shape
    n_exp = u.shape[0]
    tm = _block_rows(n, TM_PEER)
    te = TE_PEER
    assert tm % LANES == 0 and n_exp % te == 0 and te % N_KEYS == 0
    table = pltpu.VMEM((PEER_HEADS, tm // LANES, N_KEYS, LANES), F32)
    act = pltpu.VMEM((te, tm), F32)
    gate = pltpu.VMEM((te, tm), BF)
    nb = n_exp // te
    return pl.pallas_call(
        _peer_body,
        out_shape=jax.ShapeDtypeStruct((n, d), F32),
        grid=(n // tm, nb + 2),
        in_specs=[pl.BlockSpec((tm, d), lambda i, j: (i, 0)),
                  pl.BlockSpec((1, d), lambda i, j: (0, 0)),
                  pl.BlockSpec(wqt.shape, lambda i, j: (0, 0)),
                  pl.BlockSpec(keys.shape, lambda i, j: (0, 0, 0)),
                  pl.BlockSpec((te, d), lambda i, j: (jnp.minimum(j, nb - 1), 0)),
                  pl.BlockSpec((d, te), lambda i, j: (0, jnp.clip(j - 2, 0, nb - 1)))],
        out_specs=pl.BlockSpec((tm, d), lambda i, j: (i, 0)),
        scratch_shapes=[pltpu.VMEM((d, tm), BF), pltpu.VMEM((wqt.shape[0], tm), BF),
                        pltpu.VMEM((d, tm), F32), act, act, gate, gate,
                        table, table, table, table],
        compiler_params=_params(("parallel", "arbitrary")),
        name="peer",
    )(x, g.reshape(1, d), wqt, keys, u, vt)


def kernel(x_prompt, x_sample, cache_k, cache_v, page_table, a_norm_g, a_w_in, a_vnorm_g, a_w_s, a_b_s, a_w_out,
           kv_norm_g, w_k, w_v, k_norm_g, b_norm_g, w_q, q_norm_g, lambda_q1, lambda_k1, lambda_q2, lambda_k2,
           subln_g, w_o, f_norm_g, peer_w_q, peer_keys, peer_u, peer_v):
    bsz, seq, d = x_prompt.shape
    r, dec_seq, _ = x_sample.shape
    assert dec_seq == 1 and seq % CHUNK == 0
    depth = f_norm_g.shape[0]
    n_a = a_norm_g.shape[0]
    nh2 = 2 * N_HEADS
    slopes = tuple(float(v) * LOG2E for v in 2.0 ** (-8.0 * np.arange(1, N_HEADS + 1, dtype=np.float32) / N_HEADS))

    xp = x_prompt.reshape(bsz * seq, d)
    xs = x_sample.reshape(r, d)
    seg = _seg_mean_matrix(nh2 * HEAD_DIM, HEAD_DIM)
    gmlp_v_rows = []
    for l in range(depth):
        if l < n_a:
            win = a_w_in[l].astype(BF)
            wout = a_w_out[l].astype(BF)
            xp = _gmlp_prompt(xp, a_norm_g[l], win, a_vnorm_g[l], a_w_s[l], a_b_s[l], wout)
            xs, vs = _gmlp_sample(xs, a_norm_g[l], win, a_vnorm_g[l], a_w_s[l], a_b_s[l], wout)
            gmlp_v_rows.append(vs.reshape(r, dec_seq, -1))
        else:
            if l == n_a:
                wk = w_k.astype(BF)
                wv = w_v.astype(BF)
                kg = jnp.tile(k_norm_g, nh2).reshape(1, nh2 * HEAD_DIM)
                kp, vp, kp_b, vp_b = _shared_kv(xp, kv_norm_g, wk, wv, seg, kg)
                ks, vs_new, _, _ = _shared_kv(xs, kv_norm_g, wk, wv, seg, kg)
                k_hm = kp_b.reshape(bsz, seq, nh2, HEAD_DIM).transpose(0, 2, 1, 3)
                v_b = vp_b.reshape(bsz, seq, N_HEADS * V_DIM)
            i = l - n_a
            lam_init = 0.8 - 0.6 * math.exp(-0.3 * l)
            wq = w_q[i].astype(BF)
            qg = jnp.tile(q_norm_g[i], nh2).reshape(1, nh2 * HEAD_DIM)
            lams = [v[i].reshape(1, HEAD_DIM) for v in (lambda_q1, lambda_k1, lambda_q2, lambda_k2)]
            sg = subln_g[i].reshape(1, V_DIM)
            wo = w_o[i].astype(BF)
            qp = _q_proj(xp, b_norm_g[i], wq, seg, qg)
            q_hm = qp.reshape(bsz, seq, nh2, HEAD_DIM).transpose(0, 2, 1, 3)
            op = _prompt_attention(q_hm, k_hm, v_b, lams, sg, lam_init, slopes)
            xp = _o_proj(xp, op.reshape(bsz * seq, -1), wo)
            qs = _q_proj(xs, b_norm_g[i], wq, seg, qg)
            os_ = _decode_attention(qs, ks, vs_new, cache_k, cache_v, page_table, lams, sg, lam_init, slopes)
            xs = _o_proj(xs, os_, wo)
        wqt = peer_w_q[l].T.astype(BF)
        keys = peer_keys[l].reshape(PEER_HEADS * 2, N_KEYS, -1).astype(BF)
        u = peer_u[l].astype(BF)
        vt = peer_v[l].T.astype(BF)
        xp = _peer(xp, f_norm_g[l], wqt, keys, u, vt)
        xs = _peer(xs, f_norm_g[l], wqt, keys, u, vt)

    state_gmlp_v = jnp.stack(gmlp_v_rows, axis=0)
    return (xp.reshape(bsz, seq, d), xs.reshape(r, dec_seq, d),
            kp.reshape(bsz, seq, nh2, HEAD_DIM), vp.reshape(bsz, seq, N_HEADS, V_DIM),
            ks.reshape(r, dec_seq, nh2, HEAD_DIM), vs_new.reshape(r, dec_seq, N_HEADS, V_DIM),
            state_gmlp_v)
```

```python
import functools
import math

import numpy as np
import jax
import jax.numpy as jnp
from jax import lax
from jax.experimental import pallas as pl
from jax.experimental.pallas import tpu as pltpu

F32 = jnp.float32
BF = jnp.bfloat16

EPS = 1e-6
MASK_VALUE = -1e9
M_INIT = -1e30
LOG2E = math.log2(math.e)

LANES = 128
CHUNK = 128
N_GROUPS = 8
N_HEADS = 8
HEAD_DIM = 64
V_DIM = 128
PAGE = 128
PEER_HEADS = 8
N_KEYS = 128
PEER_TOPK = 16

VMEM_LIMIT = 56 * 1024 * 1024

TM_DENSE = 512
TM_PEER = 512
TE_PEER = 512
TQ_ATTN = 512
HEAD_MAPS_PER_PASS = 8
PAGES_PER_STEP = 8


def _params(sem):
    return pltpu.CompilerParams(dimension_semantics=sem, vmem_limit_bytes=VMEM_LIMIT)


def _gelu(x):
    return 0.5 * x * (1.0 + jnp.tanh(0.7978845608028654 * (x + 0.044715 * (x * x * x))))


def _rms(x, g):
    ms = jnp.mean(x * x, axis=-1, keepdims=True)
    return x * lax.rsqrt(ms + EPS) * g


def _mm(a, b):
    return jnp.dot(a, b, preferred_element_type=F32)


def _mm_nt(a, b):
    return lax.dot_general(a, b, (((1,), (1,)), ((), ())), preferred_element_type=F32)


def _row_spec(tm, d):
    return pl.BlockSpec((tm, d), lambda i: (i, 0))


def _full_spec(shape):
    nd = len(shape)
    return pl.BlockSpec(shape, lambda *_: (0,) * nd)


def _block_rows(n, pref):
    tm = min(pref, n)
    assert n % tm == 0, (n, tm)
    return tm


def _gmlp_front(x_ref, ng_ref, win_ref, vg_ref):
    x = x_ref[...]
    xn = _rms(x, ng_ref[...]).astype(BF)
    z = _gelu(_mm(xn, win_ref[...]))
    gd = z.shape[1] // 2
    return x, z[:, :gd], _rms(z[:, gd:], vg_ref[...])


def _gmlp_prompt_body(x_ref, ng_ref, win_ref, vg_ref, ws_ref, bias_ref, wout_ref, xo_ref, gated_ref):
    x, u, vn = _gmlp_front(x_ref, ng_ref, win_ref, vg_ref)
    vb = vn.astype(BF)
    tm, gd = u.shape
    gw = gd // N_GROUPS
    row = lax.broadcasted_iota(jnp.int32, (CHUNK, CHUNK), 0)
    col = lax.broadcasted_iota(jnp.int32, (CHUNK, CHUNK), 1)
    causal = col <= row
    for g in range(N_GROUPS):
        w = jnp.where(causal, ws_ref[g], 0.0).astype(BF)
        for c in range(tm // CHUNK):
            rs = slice(c * CHUNK, (c + 1) * CHUNK)
            cs = slice(g * gw, (g + 1) * gw)
            mixed = _mm(w, vb[rs, cs]) + bias_ref[:, cs]
            gated_ref[rs, cs] = (u[rs, cs] * mixed).astype(BF)
    xo_ref[...] = x + _mm(gated_ref[...], wout_ref[...])


def _gmlp_sample_body(x_ref, ng_ref, win_ref, vg_ref, w0_ref, b0_ref, wout_ref, xo_ref, vn_ref):
    x, u, vn = _gmlp_front(x_ref, ng_ref, win_ref, vg_ref)
    vn_ref[...] = vn
    gated = (u * (vn * w0_ref[...] + b0_ref[...])).astype(BF)
    xo_ref[...] = x + _mm(gated, wout_ref[...])


def _gmlp_prompt(x, ng, win, vg, ws, bs, wout):
    n, d = x.shape
    gd = wout.shape[0]
    tm = _block_rows(n, TM_DENSE)
    assert tm % CHUNK == 0
    bias = jnp.repeat(bs.T, gd // N_GROUPS, axis=1)
    return pl.pallas_call(
        _gmlp_prompt_body,
        out_shape=jax.ShapeDtypeStruct((n, d), F32),
        grid=(n // tm,),
        in_specs=[_row_spec(tm, d), _full_spec((1, d)), _full_spec(win.shape), _full_spec((1, gd)),
                  _full_spec(ws.shape), _full_spec(bias.shape), _full_spec(wout.shape)],
        out_specs=_row_spec(tm, d),
        scratch_shapes=[pltpu.VMEM((tm, gd), BF)],
        compiler_params=_params(("parallel",)),
        name="gmlp_prompt",
    )(x, ng.reshape(1, d), win, vg.reshape(1, gd), ws, bias, wout)


def _gmlp_sample(x, ng, win, vg, ws, bs, wout):
    n, d = x.shape
    gd = wout.shape[0]
    gw = gd // N_GROUPS
    tm = _block_rows(n, TM_DENSE)
    w0 = jnp.repeat(ws[:, 0, 0], gw).reshape(1, gd)
    b0 = jnp.repeat(bs[:, 0], gw).reshape(1, gd)
    return pl.pallas_call(
        _gmlp_sample_body,
        out_shape=(jax.ShapeDtypeStruct((n, d), F32), jax.ShapeDtypeStruct((n, gd), F32)),
        grid=(n // tm,),
        in_specs=[_row_spec(tm, d), _full_spec((1, d)), _full_spec(win.shape), _full_spec((1, gd)),
                  _full_spec((1, gd)), _full_spec((1, gd)), _full_spec(wout.shape)],
        out_specs=(_row_spec(tm, d), _row_spec(tm, gd)),
        compiler_params=_params(("parallel",)),
        name="gmlp_sample",
    )(x, ng.reshape(1, d), win, vg.reshape(1, gd), w0, b0, wout)


def _head_norm(y, seg_ref, gt):
    y2 = y * y
    hi = y2.astype(BF)
    lo = (y2 - hi.astype(F32)).astype(BF)
    ms = _mm(hi, seg_ref[...]) + _mm(lo, seg_ref[...])
    return y * lax.rsqrt(ms + EPS) * gt


def _seg_mean_matrix(d, seg):
    idx = np.arange(d) // seg
    return jnp.asarray((idx[:, None] == idx[None, :]).astype(np.float32) / seg, BF)


def _kv_body(x_ref, g_ref, wk_ref, wv_ref, seg_ref, kg_ref, k_ref, v_ref, kb_ref, vb_ref):
    h = _rms(x_ref[...], g_ref[...]).astype(BF)
    k = _head_norm(_mm(h, wk_ref[...]), seg_ref, kg_ref[...])
    v = _mm(h, wv_ref[...])
    k_ref[...] = k
    v_ref[...] = v
    kb_ref[...] = k.astype(BF)
    vb_ref[...] = v.astype(BF)


def _shared_kv(x, g, wk, wv, seg, kg):
    n, d = x.shape
    dk, dv = wk.shape[1], wv.shape[1]
    tm = _block_rows(n, TM_DENSE)
    return pl.pallas_call(
        _kv_body,
        out_shape=(jax.ShapeDtypeStruct((n, dk), F32), jax.ShapeDtypeStruct((n, dv), F32),
                   jax.ShapeDtypeStruct((n, dk), BF), jax.ShapeDtypeStruct((n, dv), BF)),
        grid=(n // tm,),
        in_specs=[_row_spec(tm, d), _full_spec((1, d)), _full_spec(wk.shape), _full_spec(wv.shape),
                  _full_spec(seg.shape), _full_spec((1, dk))],
        out_specs=(_row_spec(tm, dk), _row_spec(tm, dv), _row_spec(tm, dk), _row_spec(tm, dv)),
        compiler_params=_params(("parallel",)),
        name="shared_kv",
    )(x, g.reshape(1, d), wk, wv, seg, kg)


def _q_body(x_ref, g_ref, wq_ref, seg_ref, qg_ref, q_ref):
    h = _rms(x_ref[...], g_ref[...]).astype(BF)
    q = _head_norm(_mm(h, wq_ref[...]), seg_ref, qg_ref[...])
    q_ref[...] = (q * (HEAD_DIM ** -0.5 * LOG2E)).astype(BF)


def _q_proj(x, g, wq, seg, qg):
    n, d = x.shape
    dq = wq.shape[1]
    tm = _block_rows(n, TM_DENSE)
    return pl.pallas_call(
        _q_body,
        out_shape=jax.ShapeDtypeStruct((n, dq), BF),
        grid=(n // tm,),
        in_specs=[_row_spec(tm, d), _full_spec((1, d)), _full_spec(wq.shape), _full_spec(seg.shape),
                  _full_spec((1, dq))],
        out_specs=_row_spec(tm, dq),
        compiler_params=_params(("parallel",)),
        name="q_proj",
    )(x, g.reshape(1, d), wq, seg, qg)


def _oproj_body(x_ref, o_ref, w_ref, xo_ref):
    xo_ref[...] = x_ref[...] + _mm(o_ref[...], w_ref[...])


def _o_proj(x, o, w):
    n, d = x.shape
    tm = _block_rows(n, TM_DENSE)
    return pl.pallas_call(
        _oproj_body,
        out_shape=jax.ShapeDtypeStruct((n, d), F32),
        grid=(n // tm,),
        in_specs=[_row_spec(tm, d), _row_spec(tm, o.shape[1]), _full_spec(w.shape)],
        out_specs=_row_spec(tm, d),
        compiler_params=_params(("parallel",)),
        name="o_proj",
    )(x, o, w)


def _lambda(lq1, lk1, lq2, lk2, lam_init):
    a = jnp.sum(lq1[...] * lk1[...], axis=-1, keepdims=True)
    b = jnp.sum(lq2[...] * lk2[...], axis=-1, keepdims=True)
    return jnp.exp(a) - jnp.exp(b) + lam_init


def _sub_norm(o, sg, lam_init):
    ms = jnp.mean(o * o, axis=-1, keepdims=True)
    return o * lax.rsqrt(ms + EPS) * sg * (1.0 - lam_init)


def _flash_body(tq_ref, tk_ref, q_ref, k_ref, v_ref, lq1, lk1, lq2, lk2, sg_ref, o_ref,
                m_ref, acc_ref, s_ref, p_ref, *, lam_init, slopes, tq):
    t = pl.program_id(1)
    qi = tq_ref[t]
    kj = tk_ref[t]
    nh2 = q_ref.shape[0]
    group = s_ref.shape[0]

    @pl.when(kj == 0)
    def _():
        m_ref[...] = jnp.full(m_ref.shape, M_INIT, F32)
        acc_ref[...] = jnp.zeros(acc_ref.shape, F32)

    def step(diagonal):
        kpos = (lax.broadcasted_iota(jnp.int32, (1, tq), 1) + (kj - qi) * tq).astype(F32)
        if diagonal:
            row = lax.broadcasted_iota(jnp.int32, (tq, tq), 0)
            col = lax.broadcasted_iota(jnp.int32, (tq, tq), 1)
            allowed = col <= row
        ones = jnp.ones((tq, V_DIM), BF)
        heads_per_pass = group // 2
        for h0 in range(0, N_HEADS, heads_per_pass):
            mhs = [m * N_HEADS + h for h in range(h0, h0 + heads_per_pass) for m in range(2)]
            for idx, mh in enumerate(mhs):
                s_ref[idx] = _mm_nt(q_ref[mh], k_ref[mh])
            for idx, mh in enumerate(mhs):
                s = s_ref[idx] + slopes[mh % N_HEADS] * kpos
                if diagonal:
                    s = jnp.where(allowed, s, MASK_VALUE)
                m_old = m_ref[mh]
                m_new = jnp.maximum(m_old, jnp.max(s, axis=1, keepdims=True))
                alpha = jnp.exp2(m_old - m_new)
                p_ref[idx] = jnp.exp2(s - jnp.tile(m_new, (1, tq // LANES))).astype(BF)
                m_ref[mh] = m_new
                acc_ref[mh] = jnp.tile(alpha, (1, 2)) * acc_ref[mh]
            for idx, mh in enumerate(mhs):
                h = mh % N_HEADS
                v_ext = jnp.concatenate([v_ref[:, h * V_DIM:(h + 1) * V_DIM], ones], axis=1)
                acc_ref[mh] += _mm(p_ref[idx], v_ext)

    @pl.when(kj < qi)
    def _():
        step(False)

    @pl.when(kj == qi)
    def _():
        step(True)
        lam = _lambda(lq1, lk1, lq2, lk2, lam_init)
        for h in range(N_HEADS):
            o0 = acc_ref[h, :, :V_DIM] / acc_ref[h, :, V_DIM:]
            o1 = acc_ref[N_HEADS + h, :, :V_DIM] / acc_ref[N_HEADS + h, :, V_DIM:]
            o = _sub_norm(o0 - lam * o1, sg_ref[...], lam_init)
            o_ref[:, h * V_DIM:(h + 1) * V_DIM] = o.astype(BF)


def _prompt_attention(q_hm, k_hm, v, lams, sg, lam_init, slopes):
    b, nh2, t, hd = q_hm.shape
    dv = v.shape[2]
    tq = _block_rows(t, TQ_ATTN)
    nq = t // tq
    tri_q = np.asarray([qi for qi in range(nq) for _ in range(qi + 1)], np.int32)
    tri_k = np.asarray([kj for qi in range(nq) for kj in range(qi + 1)], np.int32)

    def const_spec(shape):
        return pl.BlockSpec(shape, lambda bi, ti, tqr, tkr: (0,) * len(shape))

    body = functools.partial(_flash_body, lam_init=lam_init, slopes=slopes, tq=tq)
    grid_spec = pltpu.PrefetchScalarGridSpec(
        num_scalar_prefetch=2,
        grid=(b, len(tri_q)),
        in_specs=[pl.BlockSpec((None, nh2, tq, hd), lambda bi, ti, tqr, tkr: (bi, 0, tqr[ti], 0)),
                  pl.BlockSpec((None, nh2, tq, hd), lambda bi, ti, tqr, tkr: (bi, 0, tkr[ti], 0)),
                  pl.BlockSpec((None, tq, dv), lambda bi, ti, tqr, tkr: (bi, tkr[ti], 0)),
                  *[const_spec((1, hd))] * 4, const_spec((1, V_DIM))],
        out_specs=pl.BlockSpec((None, tq, dv), lambda bi, ti, tqr, tkr: (bi, tqr[ti], 0)),
        scratch_shapes=[pltpu.VMEM((nh2, tq, LANES), F32), pltpu.VMEM((nh2, tq, 2 * V_DIM), F32),
                        pltpu.VMEM((HEAD_MAPS_PER_PASS, tq, tq), F32),
                        pltpu.VMEM((HEAD_MAPS_PER_PASS, tq, tq), BF)],
    )
    return pl.pallas_call(
        body,
        out_shape=jax.ShapeDtypeStruct((b, t, dv), BF),
        grid_spec=grid_spec,
        compiler_params=_params(("parallel", "arbitrary")),
        name="prompt_attention",
    )(jnp.asarray(tri_q), jnp.asarray(tri_k), q_hm, k_hm, v, *lams, sg)


def _decode_body(pt_ref, q_ref, kn_ref, vn_ref, slope_ref, expand_ref, *rest, lam_init, past_len, pps):
    kt_refs = rest[:pps]
    vp_refs = rest[pps:2 * pps]
    lq1, lk1, lq2, lk2, sg_ref, o_ref, qb_ref, m_ref, l_ref, acc_ref = rest[2 * pps:]
    g = pl.program_id(1)
    nh2, hd = q_ref.shape

    @pl.when(g == 0)
    def _():
        eye = (lax.broadcasted_iota(jnp.int32, (hd, hd), 0) == lax.broadcasted_iota(jnp.int32, (hd, hd), 1))
        ones = jnp.ones((hd, LANES), BF)
        q = q_ref[...].astype(F32)
        for mh in range(nh2):
            qb_ref[mh] = _mm(jnp.where(eye, q[mh:mh + 1, :], 0.0).astype(BF), ones)
        s_self = jnp.sum(q * kn_ref[...], axis=1, keepdims=True)
        m_ref[...] = jnp.broadcast_to(s_self, m_ref.shape)
        l_ref[...] = jnp.ones(l_ref.shape, F32)
        acc_ref[...] = jnp.concatenate([vn_ref[...], vn_ref[...]], axis=0)

    row = lax.broadcasted_iota(jnp.int32, (nh2, PAGE * N_HEADS), 0)
    col = lax.broadcasted_iota(jnp.int32, (nh2, PAGE * N_HEADS), 1)
    own_head = row % N_HEADS == col % N_HEADS
    for jj in range(pps):
        first_key = (g * pps + jj) * PAGE - past_len
        kpos = (lax.broadcasted_iota(jnp.int32, (1, PAGE), 1) + first_key).astype(F32)
        s = jnp.sum(kt_refs[jj][...] * qb_ref[...], axis=1) + slope_ref[...] * kpos
        m_old = m_ref[...]
        m_new = jnp.maximum(m_old, jnp.max(s, axis=1, keepdims=True))
        alpha = jnp.exp2(m_old - m_new)
        p = jnp.exp2(s - m_new)
        l_ref[...] = alpha * l_ref[...] + jnp.sum(p, axis=1, keepdims=True)
        m_ref[...] = m_new
        p_rows = _mm(p.astype(BF), expand_ref[...])
        p_rows = jnp.where(own_head, p_rows, 0.0).astype(BF)
        v_rows = vp_refs[jj][...].reshape(PAGE * N_HEADS, V_DIM).astype(BF)
        acc_ref[...] = alpha * acc_ref[...] + _mm(p_rows, v_rows)

    @pl.when(g == pl.num_programs(1) - 1)
    def _():
        lam = _lambda(lq1, lk1, lq2, lk2, lam_init)
        o0 = acc_ref[:N_HEADS, :] / l_ref[:N_HEADS, :]
        o1 = acc_ref[N_HEADS:, :] / l_ref[N_HEADS:, :]
        o_ref[...] = _sub_norm(o0 - lam * o1, sg_ref[...], lam_init).astype(BF)


def _decode_attention(q, k_new, v_new, cache_k, cache_v, page_table, lams, sg, lam_init, slopes):
    r, d = q.shape
    n_pages = page_table.shape[1]
    pps = min(PAGES_PER_STEP, n_pages)
    assert n_pages % pps == 0 and cache_k.shape[1] == PAGE
    nh2 = 2 * N_HEADS
    slope_arr = jnp.asarray(np.tile(np.asarray(slopes, np.float32), 2)[:, None] * np.ones((1, LANES), np.float32))
    expand = jnp.asarray(np.repeat(np.eye(PAGE, dtype=np.float32), N_HEADS, axis=1), BF)
    cache_kt = jnp.transpose(cache_k, (0, 2, 3, 1))

    def k_spec(jj):
        return pl.BlockSpec((None, nh2, HEAD_DIM, PAGE), lambda ri, gi, pt: (pt[ri, gi * pps + jj], 0, 0, 0))

    def v_spec(jj):
        return pl.BlockSpec((None, PAGE, N_HEADS, V_DIM), lambda ri, gi, pt: (pt[ri, gi * pps + jj], 0, 0, 0))

    def tok_spec(rows, cols):
        return pl.BlockSpec((None, rows, cols), lambda ri, gi, pt: (ri, 0, 0))

    def const_spec(shape):
        return pl.BlockSpec(shape, lambda ri, gi, pt: (0,) * len(shape))

    body = functools.partial(_decode_body, lam_init=lam_init, past_len=n_pages * PAGE, pps=pps)
    grid_spec = pltpu.PrefetchScalarGridSpec(
        num_scalar_prefetch=1,
        grid=(r, n_pages // pps),
        in_specs=[tok_spec(nh2, HEAD_DIM), tok_spec(nh2, HEAD_DIM), tok_spec(N_HEADS, V_DIM),
                  const_spec((nh2, LANES)), const_spec(expand.shape),
                  *[k_spec(jj) for jj in range(pps)], *[v_spec(jj) for jj in range(pps)],
                  *[const_spec((1, HEAD_DIM))] * 4, const_spec((1, V_DIM))],
        out_specs=tok_spec(N_HEADS, V_DIM),
        scratch_shapes=[pltpu.VMEM((nh2, HEAD_DIM, LANES), F32),
                        pltpu.VMEM((nh2, LANES), F32), pltpu.VMEM((nh2, LANES), F32),
                        pltpu.VMEM((nh2, V_DIM), F32)],
    )
    out = pl.pallas_call(
        body,
        out_shape=jax.ShapeDtypeStruct((r, N_HEADS, V_DIM), BF),
        grid_spec=grid_spec,
        compiler_params=_params(("parallel", "arbitrary")),
        name="decode_attention",
    )(page_table, q.reshape(r, nh2, HEAD_DIM), k_new.reshape(r, nh2, HEAD_DIM),
      v_new.reshape(r, N_HEADS, V_DIM), slope_arr, expand, *([cache_kt] * pps), *([cache_v] * pps), *lams, sg)
    return out.reshape(r, d)


def _top_rank(s, k, order=None):
    if order is None:
        order = lax.broadcasted_iota(jnp.int32, s.shape, 0).astype(F32)
    work = s
    rank = jnp.full(s.shape, float(k), F32)
    vals = []
    for a in range(k):
        m = jnp.max(work, axis=0, keepdims=True)
        first = jnp.min(jnp.where(work == m, order, jnp.inf), axis=0, keepdims=True)
        hit = order == first
        rank = jnp.where(hit, float(a), rank)
        work = jnp.where(hit, -jnp.inf, work)
        vals.append(m)
    return vals, rank


def _top_values(s, k):
    work = s
    vals = []
    for _ in range(k):
        m = jnp.max(work, axis=0, keepdims=True)
        work = jnp.where(work == m, -jnp.inf, work)
        vals.append(m)
    taken = jnp.sum(jnp.where(work != s, 1.0, 0.0), axis=0, keepdims=True)
    return vals, taken == float(k)


KQ = 4
SELECT_UNROLL = 2


def _candidates(v1, v2, k):
    assert KQ * KQ >= k
    v1_all = jnp.concatenate(v1, axis=0)
    v2_all = jnp.concatenate(v2, axis=0)
    ar = lax.broadcasted_iota(jnp.int32, v1_all.shape, 0)
    low_a = jnp.where(ar >= KQ, v1_all, -jnp.inf)
    return jnp.concatenate([v1[a] + v2_all for a in range(KQ)] + [low_a + v2[b] for b in range(KQ)], axis=0)


def _partner_counts(picked, k):
    count_hi = picked[KQ * k:(KQ + 1) * k]
    for b in range(1, KQ):
        count_hi = count_hi + picked[(KQ + b) * k:(KQ + b + 1) * k]
    return [jnp.sum(picked[a * k:(a + 1) * k], axis=0, keepdims=True) if a < KQ else count_hi[a:a + 1]
            for a in range(k)]


def _store_gate_tables(h, ts, a_ref, n_ref, e2_ref, r2_ref, s1, s2, max1, max2, top_sums, n, r2):
    z = jnp.ones_like(top_sums[0])
    for t in range(1, len(top_sums)):
        z = z + jnp.exp(top_sums[t] - top_sums[0])
    a_ref[h, ts] = jnp.exp(s1 - max1) / z
    n_ref[h, ts] = n
    e2_ref[h, ts] = jnp.exp(s2 - max2).astype(e2_ref.dtype)
    r2_ref[h, ts] = r2.astype(r2_ref.dtype)


def _half_scores(h, ts, keys_ref, qt_ref):
    lanes = pl.ds(pl.multiple_of(ts * LANES, LANES), LANES)

    def scores(c):
        hc = 2 * h + c
        qq = qt_ref[pl.ds(pl.multiple_of(hc * N_KEYS, N_KEYS), N_KEYS), lanes]
        return _mm(keys_ref[hc], qq)

    return scores(0), scores(1)


def _peer_select(h, ts, keys_ref, qt_ref, a_ref, n_ref, e2_ref, r2_ref):
    k = PEER_TOPK
    s1, s2 = _half_scores(h, ts, keys_ref, qt_ref)
    v1, distinct1 = _top_values(s1, k)
    v2, distinct2 = _top_values(s2, k)
    cand = _candidates(v1, v2, k)
    vc, distinct_c = _top_values(cand, k)
    counts = _partner_counts(jnp.where(cand >= vc[k - 1], 1.0, 0.0), k)
    n = jnp.zeros_like(s1)
    r2 = jnp.full(s2.shape, float(k), F32)
    for a in range(k):
        n = jnp.where(s1 == v1[a], counts[a], n)
        r2 = jnp.where(s2 == v2[a], float(a), r2)
    _store_gate_tables(h, ts, a_ref, n_ref, e2_ref, r2_ref, s1, s2, v1[0], v2[0], vc, n, r2)
    return jnp.min(jnp.where(distinct1 & distinct2 & distinct_c, 1.0, 0.0)) > 0.5


def _peer_select_ranked(h, ts, keys_ref, qt_ref, a_ref, n_ref, e2_ref, r2_ref):
    k = PEER_TOPK
    s1, s2 = _half_scores(h, ts, keys_ref, qt_ref)
    v1, r1 = _top_rank(s1, k)
    v2, r2 = _top_rank(s2, k)
    cand = _candidates(v1, v2, k)
    row = lax.broadcasted_iota(jnp.int32, cand.shape, 0)
    rb = row - KQ * k
    order = jnp.where(row < KQ * k, row, (rb % k) * k + rb // k).astype(F32)
    vc, rc = _top_rank(cand, k, order)
    counts = _partner_counts(jnp.where(rc < float(k), 1.0, 0.0), k)
    n = jnp.zeros_like(s1)
    for a in range(k):
        n = jnp.where(r1 == float(a), counts[a], n)
    _store_gate_tables(h, ts, a_ref, n_ref, e2_ref, r2_ref, s1, s2, v1[0], v2[0], vc, n, r2)


def _peer_body(x_ref, g_ref, wqt_ref, keys_ref, u_ref, vt_ref, o_ref,
               ht_ref, qt_ref, acc_ref, act0_ref, act1_ref, gate0_ref, gate1_ref, a_ref, n_ref, e2_ref, r2_ref):
    s = pl.program_id(1)
    n_blocks = pl.num_programs(1) - 2
    tm = x_ref.shape[0]
    te = u_ref.shape[0]
    nts = tm // LANES

    @pl.when(s == 0)
    def _():
        hn = _rms(x_ref[...], g_ref[...])
        ht = hn.T.astype(BF)
        ht_ref[...] = ht
        qt_ref[...] = _mm(wqt_ref[...], ht).astype(BF)
        acc_ref[...] = jnp.zeros(acc_ref.shape, F32)
        act1_ref[...] = jnp.zeros(act1_ref.shape, F32)
        gate0_ref[...] = jnp.zeros(gate0_ref.shape, BF)

        tables = (keys_ref, qt_ref, a_ref, n_ref, e2_ref, r2_ref)

        def select(it, carry):
            tiles = [((it * SELECT_UNROLL + u) // nts, (it * SELECT_UNROLL + u) % nts) for u in range(SELECT_UNROLL)]
            no_ties = [_peer_select(h, ts, *tables) for h, ts in tiles]
            for (h, ts), ok in zip(tiles, no_ties):
                @pl.when(jnp.logical_not(ok))
                def _():
                    _peer_select_ranked(h, ts, *tables)
            return carry

        assert (PEER_HEADS * nts) % SELECT_UNROLL == 0
        lax.fori_loop(0, PEER_HEADS * nts // SELECT_UNROLL, select, 0)

    gate_block = jnp.clip(s - 1, 0, n_blocks - 1)
    kdepth = 2 * N_KEYS

    def stages(act_cur, act_prev, gate_cur, gate_prev):
        for ii in range(te // N_KEYS):
            i = gate_block * (te // N_KEYS) + ii
            rs = slice(ii * N_KEYS, (ii + 1) * N_KEYS)
            act_cur[rs, :] = _mm(u_ref[rs, :], ht_ref[...])
            for lt in range(nts):
                ls = slice(lt * LANES, (lt + 1) * LANES)
                w = None
                for h in range(PEER_HEADS):
                    a_row = a_ref[h, lt, pl.ds(i, 1), :].astype(BF)
                    n_row = n_ref[h, lt, pl.ds(i, 1), :].astype(BF)
                    t = jnp.where(r2_ref[h, lt] < n_row, e2_ref[h, lt] * a_row, jnp.zeros((), BF))
                    w = t if w is None else w + t
                gate_cur[rs, ls] = _gelu(act_prev[rs, ls]).astype(BF) * w
            if (ii + 1) * N_KEYS % kdepth == 0:
                ks = slice((ii + 1) * N_KEYS - kdepth, (ii + 1) * N_KEYS)
                acc_ref[...] += _mm(vt_ref[:, ks], gate_prev[ks, :])

    @pl.when(s % 2 == 0)
    def _():
        stages(act0_ref, act1_ref, gate1_ref, gate0_ref)

    @pl.when(s % 2 == 1)
    def _():
        stages(act1_ref, act0_ref, gate0_ref, gate1_ref)

    @pl.when(s == pl.num_programs(1) - 1)
    def _():
        o_ref[...] = x_ref[...] + acc_ref[...].T


def _peer(x, g, wqt, keys, u, vt):
    n, d = x.shape
    n_exp = u.shape[0]
    tm = _block_rows(n, TM_PEER)
    te = TE_PEER
    assert tm % LANES == 0 and n_exp % te == 0 and te % N_KEYS == 0
    table = pltpu.VMEM((PEER_HEADS, tm // LANES, N_KEYS, LANES), F32)
    table_bf = pltpu.VMEM((PEER_HEADS, tm // LANES, N_KEYS, LANES), BF)
    act = pltpu.VMEM((te, tm), F32)
    gate = pltpu.VMEM((te, tm), BF)
    nb = n_exp // te
    return pl.pallas_call(
        _peer_body,
        out_shape=jax.ShapeDtypeStruct((n, d), F32),
        grid=(n // tm, nb + 2),
        in_specs=[pl.BlockSpec((tm, d), lambda i, j: (i, 0)),
                  pl.BlockSpec((1, d), lambda i, j: (0, 0)),
                  pl.BlockSpec(wqt.shape, lambda i, j: (0, 0)),
                  pl.BlockSpec(keys.shape, lambda i, j: (0, 0, 0)),
                  pl.BlockSpec((te, d), lambda i, j: (jnp.minimum(j, nb - 1), 0)),
                  pl.BlockSpec((d, te), lambda i, j: (0, jnp.clip(j - 2, 0, nb - 1)))],
        out_specs=pl.BlockSpec((tm, d), lambda i, j: (i, 0)),
        scratch_shapes=[pltpu.VMEM((d, tm), BF), pltpu.VMEM((wqt.shape[0], tm), BF),
                        pltpu.VMEM((d, tm), F32), act, act, gate, gate,
                        table, table, table_bf, table_bf],
        compiler_params=_params(("parallel", "arbitrary")),
        name="peer",
    )(x, g.reshape(1, d), wqt, keys, u, vt)


def kernel(x_prompt, x_sample, cache_k, cache_v, page_table, a_norm_g, a_w_in, a_vnorm_g, a_w_s, a_b_s, a_w_out,
           kv_norm_g, w_k, w_v, k_norm_g, b_norm_g, w_q, q_norm_g, lambda_q1, lambda_k1, lambda_q2, lambda_k2,
           subln_g, w_o, f_norm_g, peer_w_q, peer_keys, peer_u, peer_v):
    bsz, seq, d = x_prompt.shape
    r, dec_seq, _ = x_sample.shape
    assert dec_seq == 1 and seq % CHUNK == 0
    depth = f_norm_g.shape[0]
    n_a = a_norm_g.shape[0]
    nh2 = 2 * N_HEADS
    slopes = tuple(float(v) * LOG2E for v in 2.0 ** (-8.0 * np.arange(1, N_HEADS + 1, dtype=np.float32) / N_HEADS))

    xp = x_prompt.reshape(bsz * seq, d)
    xs = x_sample.reshape(r, d)
    seg = _seg_mean_matrix(nh2 * HEAD_DIM, HEAD_DIM)
    gmlp_v_rows = []
    for l in range(depth):
        if l < n_a:
            win = a_w_in[l].astype(BF)
            wout = a_w_out[l].astype(BF)
            xp = _gmlp_prompt(xp, a_norm_g[l], win, a_vnorm_g[l], a_w_s[l], a_b_s[l], wout)
            xs, vs = _gmlp_sample(xs, a_norm_g[l], win, a_vnorm_g[l], a_w_s[l], a_b_s[l], wout)
            gmlp_v_rows.append(vs.reshape(r, dec_seq, -1))
        else:
            if l == n_a:
                wk = w_k.astype(BF)
                wv = w_v.astype(BF)
                kg = jnp.tile(k_norm_g, nh2).reshape(1, nh2 * HEAD_DIM)
                kp, vp, kp_b, vp_b = _shared_kv(xp, kv_norm_g, wk, wv, seg, kg)
                ks, vs_new, _, _ = _shared_kv(xs, kv_norm_g, wk, wv, seg, kg)
                k_hm = kp_b.reshape(bsz, seq, nh2, HEAD_DIM).transpose(0, 2, 1, 3)
                v_b = vp_b.reshape(bsz, seq, N_HEADS * V_DIM)
            i = l - n_a
            lam_init = 0.8 - 0.6 * math.exp(-0.3 * l)
            wq = w_q[i].astype(BF)
            qg = jnp.tile(q_norm_g[i], nh2).reshape(1, nh2 * HEAD_DIM)
            lams = [v[i].reshape(1, HEAD_DIM) for v in (lambda_q1, lambda_k1, lambda_q2, lambda_k2)]
            sg = subln_g[i].reshape(1, V_DIM)
            wo = w_o[i].astype(BF)
            qp = _q_proj(xp, b_norm_g[i], wq, seg, qg)
            q_hm = qp.reshape(bsz, seq, nh2, HEAD_DIM).transpose(0, 2, 1, 3)
            op = _prompt_attention(q_hm, k_hm, v_b, lams, sg, lam_init, slopes)
            xp = _o_proj(xp, op.reshape(bsz * seq, -1), wo)
            qs = _q_proj(xs, b_norm_g[i], wq, seg, qg)
            os_ = _decode_attention(qs, ks, vs_new, cache_k, cache_v, page_table, lams, sg, lam_init, slopes)
            xs = _o_proj(xs, os_, wo)
        wqt = peer_w_q[l].T.astype(BF)
        keys = peer_keys[l].reshape(PEER_HEADS * 2, N_KEYS, -1).astype(BF)
        u = peer_u[l].astype(BF)
        vt = peer_v[l].T.astype(BF)
        xp = _peer(xp, f_norm_g[l], wqt, keys, u, vt)
        xs = _peer(xs, f_norm_g[l], wqt, keys, u, vt)

    state_gmlp_v = jnp.stack(gmlp_v_rows, axis=0)
    return (xp.reshape(bsz, seq, d), xs.reshape(r, dec_seq, d),
            kp.reshape(bsz, seq, nh2, HEAD_DIM), vp.reshape(bsz, seq, N_HEADS, V_DIM),
            ks.reshape(r, dec_seq, nh2, HEAD_DIM), vs_new.reshape(r, dec_seq, N_HEADS, V_DIM),
            state_gmlp_v)
```

```python
import functools
import math

import numpy as np
import jax
import jax.numpy as jnp
from jax import lax
from jax.experimental import pallas as pl
from jax.experimental.pallas import tpu as pltpu

F32 = jnp.float32
BF = jnp.bfloat16

EPS = 1e-6
MASK_VALUE = -1e9
M_INIT = -1e30
LOG2E = math.log2(math.e)

LANES = 128
CHUNK = 128
N_GROUPS = 8
N_HEADS = 8
HEAD_DIM = 64
V_DIM = 128
PAGE = 128
PEER_HEADS = 8
N_KEYS = 128
PEER_TOPK = 16

VMEM_LIMIT = 56 * 1024 * 1024

TM_DENSE = 512
TM_PEER = 512
TE_PEER = 1024
TQ_ATTN = 512
HEAD_MAPS_PER_PASS = 8
PAGES_PER_STEP = 8


def _params(sem):
    return pltpu.CompilerParams(dimension_semantics=sem, vmem_limit_bytes=VMEM_LIMIT)


def _gelu(x):
    return 0.5 * x * (1.0 + jnp.tanh(0.7978845608028654 * (x + 0.044715 * (x * x * x))))


def _rms(x, g):
    ms = jnp.mean(x * x, axis=-1, keepdims=True)
    return x * lax.rsqrt(ms + EPS) * g


def _mm(a, b):
    return jnp.dot(a, b, preferred_element_type=F32)


def _mm_nt(a, b):
    return lax.dot_general(a, b, (((1,), (1,)), ((), ())), preferred_element_type=F32)


def _row_spec(tm, d):
    return pl.BlockSpec((tm, d), lambda i: (i, 0))


def _full_spec(shape):
    nd = len(shape)
    return pl.BlockSpec(shape, lambda *_: (0,) * nd)


def _block_rows(n, pref):
    tm = min(pref, n)
    assert n % tm == 0, (n, tm)
    return tm


def _gmlp_front(x_ref, ng_ref, win_ref, vg_ref):
    x = x_ref[...]
    xn = _rms(x, ng_ref[...]).astype(BF)
    z = _gelu(_mm(xn, win_ref[...]))
    gd = z.shape[1] // 2
    return x, z[:, :gd], _rms(z[:, gd:], vg_ref[...])


def _gmlp_prompt_body(x_ref, ng_ref, win_ref, vg_ref, ws_ref, bias_ref, wout_ref, xo_ref, gated_ref):
    x, u, vn = _gmlp_front(x_ref, ng_ref, win_ref, vg_ref)
    vb = vn.astype(BF)
    tm, gd = u.shape
    gw = gd // N_GROUPS
    row = lax.broadcasted_iota(jnp.int32, (CHUNK, CHUNK), 0)
    col = lax.broadcasted_iota(jnp.int32, (CHUNK, CHUNK), 1)
    causal = col <= row
    for g in range(N_GROUPS):
        w = jnp.where(causal, ws_ref[g], 0.0).astype(BF)
        for c in range(tm // CHUNK):
            rs = slice(c * CHUNK, (c + 1) * CHUNK)
            cs = slice(g * gw, (g + 1) * gw)
            mixed = _mm(w, vb[rs, cs]) + bias_ref[:, cs]
            gated_ref[rs, cs] = (u[rs, cs] * mixed).astype(BF)
    xo_ref[...] = x + _mm(gated_ref[...], wout_ref[...])


def _gmlp_sample_body(x_ref, ng_ref, win_ref, vg_ref, w0_ref, b0_ref, wout_ref, xo_ref, vn_ref):
    x, u, vn = _gmlp_front(x_ref, ng_ref, win_ref, vg_ref)
    vn_ref[...] = vn
    gated = (u * (vn * w0_ref[...] + b0_ref[...])).astype(BF)
    xo_ref[...] = x + _mm(gated, wout_ref[...])


def _gmlp_prompt(x, ng, win, vg, ws, bs, wout):
    n, d = x.shape
    gd = wout.shape[0]
    tm = _block_rows(n, TM_DENSE)
    assert tm % CHUNK == 0
    bias = jnp.repeat(bs.T, gd // N_GROUPS, axis=1)
    return pl.pallas_call(
        _gmlp_prompt_body,
        out_shape=jax.ShapeDtypeStruct((n, d), F32),
        grid=(n // tm,),
        in_specs=[_row_spec(tm, d), _full_spec((1, d)), _full_spec(win.shape), _full_spec((1, gd)),
                  _full_spec(ws.shape), _full_spec(bias.shape), _full_spec(wout.shape)],
        out_specs=_row_spec(tm, d),
        scratch_shapes=[pltpu.VMEM((tm, gd), BF)],
        compiler_params=_params(("parallel",)),
        name="gmlp_prompt",
    )(x, ng.reshape(1, d), win, vg.reshape(1, gd), ws, bias, wout)


def _gmlp_sample(x, ng, win, vg, ws, bs, wout):
    n, d = x.shape
    gd = wout.shape[0]
    gw = gd // N_GROUPS
    tm = _block_rows(n, TM_DENSE)
    w0 = jnp.repeat(ws[:, 0, 0], gw).reshape(1, gd)
    b0 = jnp.repeat(bs[:, 0], gw).reshape(1, gd)
    return pl.pallas_call(
        _gmlp_sample_body,
        out_shape=(jax.ShapeDtypeStruct((n, d), F32), jax.ShapeDtypeStruct((n, gd), F32)),
        grid=(n // tm,),
        in_specs=[_row_spec(tm, d), _full_spec((1, d)), _full_spec(win.shape), _full_spec((1, gd)),
                  _full_spec((1, gd)), _full_spec((1, gd)), _full_spec(wout.shape)],
        out_specs=(_row_spec(tm, d), _row_spec(tm, gd)),
        compiler_params=_params(("parallel",)),
        name="gmlp_sample",
    )(x, ng.reshape(1, d), win, vg.reshape(1, gd), w0, b0, wout)


def _head_norm(y, seg_ref, gt):
    y2 = y * y
    hi = y2.astype(BF)
    lo = (y2 - hi.astype(F32)).astype(BF)
    ms = _mm(hi, seg_ref[...]) + _mm(lo, seg_ref[...])
    return y * lax.rsqrt(ms + EPS) * gt


def _seg_mean_matrix(d, seg):
    idx = np.arange(d) // seg
    return jnp.asarray((idx[:, None] == idx[None, :]).astype(np.float32) / seg, BF)


def _kv_body(x_ref, g_ref, wk_ref, wv_ref, seg_ref, kg_ref, k_ref, v_ref, kb_ref, vb_ref):
    h = _rms(x_ref[...], g_ref[...]).astype(BF)
    k = _head_norm(_mm(h, wk_ref[...]), seg_ref, kg_ref[...])
    v = _mm(h, wv_ref[...])
    k_ref[...] = k
    v_ref[...] = v
    kb_ref[...] = k.astype(BF)
    vb_ref[...] = v.astype(BF)


def _shared_kv(x, g, wk, wv, seg, kg):
    n, d = x.shape
    dk, dv = wk.shape[1], wv.shape[1]
    tm = _block_rows(n, TM_DENSE)
    return pl.pallas_call(
        _kv_body,
        out_shape=(jax.ShapeDtypeStruct((n, dk), F32), jax.ShapeDtypeStruct((n, dv), F32),
                   jax.ShapeDtypeStruct((n, dk), BF), jax.ShapeDtypeStruct((n, dv), BF)),
        grid=(n // tm,),
        in_specs=[_row_spec(tm, d), _full_spec((1, d)), _full_spec(wk.shape), _full_spec(wv.shape),
                  _full_spec(seg.shape), _full_spec((1, dk))],
        out_specs=(_row_spec(tm, dk), _row_spec(tm, dv), _row_spec(tm, dk), _row_spec(tm, dv)),
        compiler_params=_params(("parallel",)),
        name="shared_kv",
    )(x, g.reshape(1, d), wk, wv, seg, kg)


def _q_body(x_ref, g_ref, wq_ref, seg_ref, qg_ref, q_ref):
    h = _rms(x_ref[...], g_ref[...]).astype(BF)
    q = _head_norm(_mm(h, wq_ref[...]), seg_ref, qg_ref[...])
    q_ref[...] = (q * (HEAD_DIM ** -0.5 * LOG2E)).astype(BF)


def _q_proj(x, g, wq, seg, qg):
    n, d = x.shape
    dq = wq.shape[1]
    tm = _block_rows(n, TM_DENSE)
    return pl.pallas_call(
        _q_body,
        out_shape=jax.ShapeDtypeStruct((n, dq), BF),
        grid=(n // tm,),
        in_specs=[_row_spec(tm, d), _full_spec((1, d)), _full_spec(wq.shape), _full_spec(seg.shape),
                  _full_spec((1, dq))],
        out_specs=_row_spec(tm, dq),
        compiler_params=_params(("parallel",)),
        name="q_proj",
    )(x, g.reshape(1, d), wq, seg, qg)


def _oproj_body(x_ref, o_ref, w_ref, xo_ref):
    xo_ref[...] = x_ref[...] + _mm(o_ref[...], w_ref[...])


def _o_proj(x, o, w):
    n, d = x.shape
    tm = _block_rows(n, TM_DENSE)
    return pl.pallas_call(
        _oproj_body,
        out_shape=jax.ShapeDtypeStruct((n, d), F32),
        grid=(n // tm,),
        in_specs=[_row_spec(tm, d), _row_spec(tm, o.shape[1]), _full_spec(w.shape)],
        out_specs=_row_spec(tm, d),
        compiler_params=_params(("parallel",)),
        name="o_proj",
    )(x, o, w)


def _lambda(lq1, lk1, lq2, lk2, lam_init):
    a = jnp.sum(lq1[...] * lk1[...], axis=-1, keepdims=True)
    b = jnp.sum(lq2[...] * lk2[...], axis=-1, keepdims=True)
    return jnp.exp(a) - jnp.exp(b) + lam_init


def _sub_norm(o, sg, lam_init):
    ms = jnp.mean(o * o, axis=-1, keepdims=True)
    return o * lax.rsqrt(ms + EPS) * sg * (1.0 - lam_init)


def _flash_body(tq_ref, tk_ref, q_ref, k_ref, v_ref, lq1, lk1, lq2, lk2, sg_ref, o_ref,
                m_ref, acc_ref, s_ref, p_ref, *, lam_init, slopes, tq):
    t = pl.program_id(1)
    qi = tq_ref[t]
    kj = tk_ref[t]
    nh2 = q_ref.shape[0]
    group = s_ref.shape[0]

    @pl.when(kj == 0)
    def _():
        m_ref[...] = jnp.full(m_ref.shape, M_INIT, F32)
        acc_ref[...] = jnp.zeros(acc_ref.shape, F32)

    def step(diagonal):
        kpos = (lax.broadcasted_iota(jnp.int32, (1, tq), 1) + (kj - qi) * tq).astype(F32)
        if diagonal:
            row = lax.broadcasted_iota(jnp.int32, (tq, tq), 0)
            col = lax.broadcasted_iota(jnp.int32, (tq, tq), 1)
            allowed = col <= row
        ones = jnp.ones((tq, V_DIM), BF)
        heads_per_pass = group // 2
        for h0 in range(0, N_HEADS, heads_per_pass):
            mhs = [m * N_HEADS + h for h in range(h0, h0 + heads_per_pass) for m in range(2)]
            for idx, mh in enumerate(mhs):
                s_ref[idx] = _mm_nt(q_ref[mh], k_ref[mh])
            for idx, mh in enumerate(mhs):
                s = s_ref[idx] + slopes[mh % N_HEADS] * kpos
                if diagonal:
                    s = jnp.where(allowed, s, MASK_VALUE)
                m_old = m_ref[mh]
                m_new = jnp.maximum(m_old, jnp.max(s, axis=1, keepdims=True))
                alpha = jnp.exp2(m_old - m_new)
                p_ref[idx] = jnp.exp2(s - jnp.tile(m_new, (1, tq // LANES))).astype(BF)
                m_ref[mh] = m_new
                acc_ref[mh] = jnp.tile(alpha, (1, 2)) * acc_ref[mh]
            for idx, mh in enumerate(mhs):
                h = mh % N_HEADS
                v_ext = jnp.concatenate([v_ref[:, h * V_DIM:(h + 1) * V_DIM], ones], axis=1)
                acc_ref[mh] += _mm(p_ref[idx], v_ext)

    @pl.when(kj < qi)
    def _():
        step(False)

    @pl.when(kj == qi)
    def _():
        step(True)
        lam = _lambda(lq1, lk1, lq2, lk2, lam_init)
        for h in range(N_HEADS):
            o0 = acc_ref[h, :, :V_DIM] / acc_ref[h, :, V_DIM:]
            o1 = acc_ref[N_HEADS + h, :, :V_DIM] / acc_ref[N_HEADS + h, :, V_DIM:]
            o = _sub_norm(o0 - lam * o1, sg_ref[...], lam_init)
            o_ref[:, h * V_DIM:(h + 1) * V_DIM] = o.astype(BF)


def _prompt_attention(q_hm, k_hm, v, lams, sg, lam_init, slopes):
    b, nh2, t, hd = q_hm.shape
    dv = v.shape[2]
    tq = _block_rows(t, TQ_ATTN)
    nq = t // tq
    tri_q = np.asarray([qi for qi in range(nq) for _ in range(qi + 1)], np.int32)
    tri_k = np.asarray([kj for qi in range(nq) for kj in range(qi + 1)], np.int32)

    def const_spec(shape):
        return pl.BlockSpec(shape, lambda bi, ti, tqr, tkr: (0,) * len(shape))

    body = functools.partial(_flash_body, lam_init=lam_init, slopes=slopes, tq=tq)
    grid_spec = pltpu.PrefetchScalarGridSpec(
        num_scalar_prefetch=2,
        grid=(b, len(tri_q)),
        in_specs=[pl.BlockSpec((None, nh2, tq, hd), lambda bi, ti, tqr, tkr: (bi, 0, tqr[ti], 0)),
                  pl.BlockSpec((None, nh2, tq, hd), lambda bi, ti, tqr, tkr: (bi, 0, tkr[ti], 0)),
                  pl.BlockSpec((None, tq, dv), lambda bi, ti, tqr, tkr: (bi, tkr[ti], 0)),
                  *[const_spec((1, hd))] * 4, const_spec((1, V_DIM))],
        out_specs=pl.BlockSpec((None, tq, dv), lambda bi, ti, tqr, tkr: (bi, tqr[ti], 0)),
        scratch_shapes=[pltpu.VMEM((nh2, tq, LANES), F32), pltpu.VMEM((nh2, tq, 2 * V_DIM), F32),
                        pltpu.VMEM((HEAD_MAPS_PER_PASS, tq, tq), F32),
                        pltpu.VMEM((HEAD_MAPS_PER_PASS, tq, tq), BF)],
    )
    return pl.pallas_call(
        body,
        out_shape=jax.ShapeDtypeStruct((b, t, dv), BF),
        grid_spec=grid_spec,
        compiler_params=_params(("parallel", "arbitrary")),
        name="prompt_attention",
    )(jnp.asarray(tri_q), jnp.asarray(tri_k), q_hm, k_hm, v, *lams, sg)


def _decode_body(pt_ref, q_ref, kn_ref, vn_ref, slope_ref, expand_ref, *rest, lam_init, past_len, pps):
    kt_refs = rest[:pps]
    vp_refs = rest[pps:2 * pps]
    lq1, lk1, lq2, lk2, sg_ref, o_ref, qb_ref, m_ref, l_ref, acc_ref = rest[2 * pps:]
    g = pl.program_id(1)
    nh2, hd = q_ref.shape

    @pl.when(g == 0)
    def _():
        eye = (lax.broadcasted_iota(jnp.int32, (hd, hd), 0) == lax.broadcasted_iota(jnp.int32, (hd, hd), 1))
        ones = jnp.ones((hd, LANES), BF)
        q = q_ref[...].astype(F32)
        for mh in range(nh2):
            qb_ref[mh] = _mm(jnp.where(eye, q[mh:mh + 1, :], 0.0).astype(BF), ones)
        s_self = jnp.sum(q * kn_ref[...], axis=1, keepdims=True)
        m_ref[...] = jnp.broadcast_to(s_self, m_ref.shape)
        l_ref[...] = jnp.ones(l_ref.shape, F32)
        acc_ref[...] = jnp.concatenate([vn_ref[...], vn_ref[...]], axis=0)

    row = lax.broadcasted_iota(jnp.int32, (nh2, PAGE * N_HEADS), 0)
    col = lax.broadcasted_iota(jnp.int32, (nh2, PAGE * N_HEADS), 1)
    own_head = row % N_HEADS == col % N_HEADS
    for jj in range(pps):
        first_key = (g * pps + jj) * PAGE - past_len
        kpos = (lax.broadcasted_iota(jnp.int32, (1, PAGE), 1) + first_key).astype(F32)
        s = jnp.sum(kt_refs[jj][...] * qb_ref[...], axis=1) + slope_ref[...] * kpos
        m_old = m_ref[...]
        m_new = jnp.maximum(m_old, jnp.max(s, axis=1, keepdims=True))
        alpha = jnp.exp2(m_old - m_new)
        p = jnp.exp2(s - m_new)
        l_ref[...] = alpha * l_ref[...] + jnp.sum(p, axis=1, keepdims=True)
        m_ref[...] = m_new
        p_rows = _mm(p.astype(BF), expand_ref[...])
        p_rows = jnp.where(own_head, p_rows, 0.0).astype(BF)
        v_rows = vp_refs[jj][...].reshape(PAGE * N_HEADS, V_DIM).astype(BF)
        acc_ref[...] = alpha * acc_ref[...] + _mm(p_rows, v_rows)

    @pl.when(g == pl.num_programs(1) - 1)
    def _():
        lam = _lambda(lq1, lk1, lq2, lk2, lam_init)
        o0 = acc_ref[:N_HEADS, :] / l_ref[:N_HEADS, :]
        o1 = acc_ref[N_HEADS:, :] / l_ref[N_HEADS:, :]
        o_ref[...] = _sub_norm(o0 - lam * o1, sg_ref[...], lam_init).astype(BF)


def _decode_attention(q, k_new, v_new, cache_k, cache_v, page_table, lams, sg, lam_init, slopes):
    r, d = q.shape
    n_pages = page_table.shape[1]
    pps = min(PAGES_PER_STEP, n_pages)
    assert n_pages % pps == 0 and cache_k.shape[1] == PAGE
    nh2 = 2 * N_HEADS
    slope_arr = jnp.asarray(np.tile(np.asarray(slopes, np.float32), 2)[:, None] * np.ones((1, LANES), np.float32))
    expand = jnp.asarray(np.repeat(np.eye(PAGE, dtype=np.float32), N_HEADS, axis=1), BF)
    cache_kt = jnp.transpose(cache_k, (0, 2, 3, 1))

    def k_spec(jj):
        return pl.BlockSpec((None, nh2, HEAD_DIM, PAGE), lambda ri, gi, pt: (pt[ri, gi * pps + jj], 0, 0, 0))

    def v_spec(jj):
        return pl.BlockSpec((None, PAGE, N_HEADS, V_DIM), lambda ri, gi, pt: (pt[ri, gi * pps + jj], 0, 0, 0))

    def tok_spec(rows, cols):
        return pl.BlockSpec((None, rows, cols), lambda ri, gi, pt: (ri, 0, 0))

    def const_spec(shape):
        return pl.BlockSpec(shape, lambda ri, gi, pt: (0,) * len(shape))

    body = functools.partial(_decode_body, lam_init=lam_init, past_len=n_pages * PAGE, pps=pps)
    grid_spec = pltpu.PrefetchScalarGridSpec(
        num_scalar_prefetch=1,
        grid=(r, n_pages // pps),
        in_specs=[tok_spec(nh2, HEAD_DIM), tok_spec(nh2, HEAD_DIM), tok_spec(N_HEADS, V_DIM),
                  const_spec((nh2, LANES)), const_spec(expand.shape),
                  *[k_spec(jj) for jj in range(pps)], *[v_spec(jj) for jj in range(pps)],
                  *[const_spec((1, HEAD_DIM))] * 4, const_spec((1, V_DIM))],
        out_specs=tok_spec(N_HEADS, V_DIM),
        scratch_shapes=[pltpu.VMEM((nh2, HEAD_DIM, LANES), F32),
                        pltpu.VMEM((nh2, LANES), F32), pltpu.VMEM((nh2, LANES), F32),
                        pltpu.VMEM((nh2, V_DIM), F32)],
    )
    out = pl.pallas_call(
        body,
        out_shape=jax.ShapeDtypeStruct((r, N_HEADS, V_DIM), BF),
        grid_spec=grid_spec,
        compiler_params=_params(("parallel", "arbitrary")),
        name="decode_attention",
    )(page_table, q.reshape(r, nh2, HEAD_DIM), k_new.reshape(r, nh2, HEAD_DIM),
      v_new.reshape(r, N_HEADS, V_DIM), slope_arr, expand, *([cache_kt] * pps), *([cache_v] * pps), *lams, sg)
    return out.reshape(r, d)


def _top_rank(s, k, order=None):
    if order is None:
        order = lax.broadcasted_iota(jnp.int32, s.shape, 0).astype(F32)
    work = s
    rank = jnp.full(s.shape, float(k), F32)
    vals = []
    for a in range(k):
        m = jnp.max(work, axis=0, keepdims=True)
        first = jnp.min(jnp.where(work == m, order, jnp.inf), axis=0, keepdims=True)
        hit = order == first
        rank = jnp.where(hit, float(a), rank)
        work = jnp.where(hit, -jnp.inf, work)
        vals.append(m)
    return vals, rank


def _top_values(s, k):
    work = s
    vals = []
    for _ in range(k):
        m = jnp.max(work, axis=0, keepdims=True)
        work = jnp.where(work == m, -jnp.inf, work)
        vals.append(m)
    taken = jnp.sum(jnp.where(work != s, 1.0, 0.0), axis=0, keepdims=True)
    return vals, taken == float(k)


KQ = 4
SELECT_UNROLL = 2


def _candidates(v1, v2, k):
    assert KQ * KQ >= k
    v1_all = jnp.concatenate(v1, axis=0)
    v2_all = jnp.concatenate(v2, axis=0)
    ar = lax.broadcasted_iota(jnp.int32, v1_all.shape, 0)
    low_a = jnp.where(ar >= KQ, v1_all, -jnp.inf)
    return jnp.concatenate([v1[a] + v2_all for a in range(KQ)] + [low_a + v2[b] for b in range(KQ)], axis=0)


def _partner_counts(picked, k):
    count_hi = picked[KQ * k:(KQ + 1) * k]
    for b in range(1, KQ):
        count_hi = count_hi + picked[(KQ + b) * k:(KQ + b + 1) * k]
    return [jnp.sum(picked[a * k:(a + 1) * k], axis=0, keepdims=True) if a < KQ else count_hi[a:a + 1]
            for a in range(k)]


def _store_gate_tables(h, ts, a_ref, n_ref, e2_ref, r2_ref, s1, s2, max1, max2, top_sums, n, r2):
    z = jnp.ones_like(top_sums[0])
    for t in range(1, len(top_sums)):
        z = z + jnp.exp(top_sums[t] - top_sums[0])
    a_ref[h, ts] = jnp.exp(s1 - max1) / z
    n_ref[h, ts] = n
    e2_ref[h, ts] = jnp.exp(s2 - max2).astype(e2_ref.dtype)
    r2_ref[h, ts] = r2.astype(r2_ref.dtype)


def _half_scores(h, ts, keys_ref, qt_ref):
    lanes = pl.ds(pl.multiple_of(ts * LANES, LANES), LANES)

    def scores(c):
        hc = 2 * h + c
        qq = qt_ref[pl.ds(pl.multiple_of(hc * N_KEYS, N_KEYS), N_KEYS), lanes]
        return _mm(keys_ref[hc], qq)

    return scores(0), scores(1)


def _peer_select(h, ts, keys_ref, qt_ref, a_ref, n_ref, e2_ref, r2_ref):
    k = PEER_TOPK
    s1, s2 = _half_scores(h, ts, keys_ref, qt_ref)
    v1, distinct1 = _top_values(s1, k)
    v2, distinct2 = _top_values(s2, k)
    cand = _candidates(v1, v2, k)
    vc, distinct_c = _top_values(cand, k)
    counts = _partner_counts(jnp.where(cand >= vc[k - 1], 1.0, 0.0), k)
    n = jnp.zeros_like(s1)
    r2 = jnp.full(s2.shape, float(k), F32)
    for a in range(k):
        n = jnp.where(s1 == v1[a], counts[a], n)
        r2 = jnp.where(s2 == v2[a], float(a), r2)
    _store_gate_tables(h, ts, a_ref, n_ref, e2_ref, r2_ref, s1, s2, v1[0], v2[0], vc, n, r2)
    return jnp.min(jnp.where(distinct1 & distinct2 & distinct_c, 1.0, 0.0)) > 0.5


def _peer_select_ranked(h, ts, keys_ref, qt_ref, a_ref, n_ref, e2_ref, r2_ref):
    k = PEER_TOPK
    s1, s2 = _half_scores(h, ts, keys_ref, qt_ref)
    v1, r1 = _top_rank(s1, k)
    v2, r2 = _top_rank(s2, k)
    cand = _candidates(v1, v2, k)
    row = lax.broadcasted_iota(jnp.int32, cand.shape, 0)
    rb = row - KQ * k
    order = jnp.where(row < KQ * k, row, (rb % k) * k + rb // k).astype(F32)
    vc, rc = _top_rank(cand, k, order)
    counts = _partner_counts(jnp.where(rc < float(k), 1.0, 0.0), k)
    n = jnp.zeros_like(s1)
    for a in range(k):
        n = jnp.where(r1 == float(a), counts[a], n)
    _store_gate_tables(h, ts, a_ref, n_ref, e2_ref, r2_ref, s1, s2, v1[0], v2[0], vc, n, r2)


def _peer_body(x_ref, g_ref, wqt_ref, keys_ref, u_ref, vt_ref, o_ref,
               ht_ref, qt_ref, acc_ref, act0_ref, act1_ref, gate0_ref, gate1_ref, a_ref, n_ref, e2_ref, r2_ref):
    s = pl.program_id(1)
    n_blocks = pl.num_programs(1) - 2
    tm = x_ref.shape[0]
    te = u_ref.shape[0]
    nts = tm // LANES

    @pl.when(s == 0)
    def _():
        hn = _rms(x_ref[...], g_ref[...])
        ht = hn.T.astype(BF)
        ht_ref[...] = ht
        qt_ref[...] = _mm(wqt_ref[...], ht).astype(BF)
        acc_ref[...] = jnp.zeros(acc_ref.shape, F32)
        act1_ref[...] = jnp.zeros(act1_ref.shape, F32)
        gate0_ref[...] = jnp.zeros(gate0_ref.shape, BF)

        tables = (keys_ref, qt_ref, a_ref, n_ref, e2_ref, r2_ref)

        def select(it, carry):
            tiles = [((it * SELECT_UNROLL + u) // nts, (it * SELECT_UNROLL + u) % nts) for u in range(SELECT_UNROLL)]
            no_ties = [_peer_select(h, ts, *tables) for h, ts in tiles]
            for (h, ts), ok in zip(tiles, no_ties):
                @pl.when(jnp.logical_not(ok))
                def _():
                    _peer_select_ranked(h, ts, *tables)
            return carry

        assert (PEER_HEADS * nts) % SELECT_UNROLL == 0
        lax.fori_loop(0, PEER_HEADS * nts // SELECT_UNROLL, select, 0)

    gate_block = jnp.clip(s - 1, 0, n_blocks - 1)
    kdepth = 2 * N_KEYS

    def stages(act_cur, act_prev, gate_cur, gate_prev):
        for ii in range(te // N_KEYS):
            i = gate_block * (te // N_KEYS) + ii
            rs = slice(ii * N_KEYS, (ii + 1) * N_KEYS)
            act_cur[rs, :] = _mm(u_ref[rs, :], ht_ref[...])
            for lt in range(nts):
                ls = slice(lt * LANES, (lt + 1) * LANES)
                w = None
                for h in range(PEER_HEADS):
                    a_row = a_ref[h, lt, pl.ds(i, 1), :].astype(BF)
                    n_row = n_ref[h, lt, pl.ds(i, 1), :].astype(BF)
                    t = jnp.where(r2_ref[h, lt] < n_row, e2_ref[h, lt] * a_row, jnp.zeros((), BF))
                    w = t if w is None else w + t
                gate_cur[rs, ls] = _gelu(act_prev[rs, ls]).astype(BF) * w
            if (ii + 1) * N_KEYS % kdepth == 0:
                ks = slice((ii + 1) * N_KEYS - kdepth, (ii + 1) * N_KEYS)
                acc_ref[...] += _mm(vt_ref[:, ks], gate_prev[ks, :])

    @pl.when(s % 2 == 0)
    def _():
        stages(act0_ref, act1_ref, gate1_ref, gate0_ref)

    @pl.when(s % 2 == 1)
    def _():
        stages(act1_ref, act0_ref, gate0_ref, gate1_ref)

    @pl.when(s == pl.num_programs(1) - 1)
    def _():
        o_ref[...] = x_ref[...] + acc_ref[...].T


def _peer(x, g, wqt, keys, u, vt):
    n, d = x.shape
    n_exp = u.shape[0]
    tm = _block_rows(n, TM_PEER)
    te = vt.shape[2]
    assert tm % LANES == 0 and n_exp % te == 0 and te % (2 * N_KEYS) == 0
    table = pltpu.VMEM((PEER_HEADS, tm // LANES, N_KEYS, LANES), F32)
    table_bf = pltpu.VMEM((PEER_HEADS, tm // LANES, N_KEYS, LANES), BF)
    act = pltpu.VMEM((te, tm), F32)
    gate = pltpu.VMEM((te, tm), BF)
    nb = n_exp // te
    return pl.pallas_call(
        _peer_body,
        out_shape=jax.ShapeDtypeStruct((n, d), F32),
        grid=(n // tm, nb + 2),
        in_specs=[pl.BlockSpec((tm, d), lambda i, j: (i, 0)),
                  pl.BlockSpec((1, d), lambda i, j: (0, 0)),
                  pl.BlockSpec(wqt.shape, lambda i, j: (0, 0)),
                  pl.BlockSpec(keys.shape, lambda i, j: (0, 0, 0)),
                  pl.BlockSpec((te, d), lambda i, j: (jnp.minimum(j, nb - 1), 0)),
                  pl.BlockSpec((None, d, te), lambda i, j: (jnp.clip(j - 2, 0, nb - 1), 0, 0))],
        out_specs=pl.BlockSpec((tm, d), lambda i, j: (i, 0)),
        scratch_shapes=[pltpu.VMEM((d, tm), BF), pltpu.VMEM((wqt.shape[0], tm), BF),
                        pltpu.VMEM((d, tm), F32), act, act, gate, gate,
                        table, table, table_bf, table_bf],
        compiler_params=_params(("parallel", "arbitrary")),
        name="peer",
    )(x, g.reshape(1, d), wqt, keys, u, vt)


def kernel(x_prompt, x_sample, cache_k, cache_v, page_table, a_norm_g, a_w_in, a_vnorm_g, a_w_s, a_b_s, a_w_out,
           kv_norm_g, w_k, w_v, k_norm_g, b_norm_g, w_q, q_norm_g, lambda_q1, lambda_k1, lambda_q2, lambda_k2,
           subln_g, w_o, f_norm_g, peer_w_q, peer_keys, peer_u, peer_v):
    bsz, seq, d = x_prompt.shape
    r, dec_seq, _ = x_sample.shape
    assert dec_seq == 1 and seq % CHUNK == 0
    depth = f_norm_g.shape[0]
    n_a = a_norm_g.shape[0]
    nh2 = 2 * N_HEADS
    slopes = tuple(float(v) * LOG2E for v in 2.0 ** (-8.0 * np.arange(1, N_HEADS + 1, dtype=np.float32) / N_HEADS))

    xp = x_prompt.reshape(bsz * seq, d)
    xs = x_sample.reshape(r, d)
    seg = _seg_mean_matrix(nh2 * HEAD_DIM, HEAD_DIM)
    gmlp_v_rows = []
    for l in range(depth):
        if l < n_a:
            win = a_w_in[l].astype(BF)
            wout = a_w_out[l].astype(BF)
            xp = _gmlp_prompt(xp, a_norm_g[l], win, a_vnorm_g[l], a_w_s[l], a_b_s[l], wout)
            xs, vs = _gmlp_sample(xs, a_norm_g[l], win, a_vnorm_g[l], a_w_s[l], a_b_s[l], wout)
            gmlp_v_rows.append(vs.reshape(r, dec_seq, -1))
        else:
            if l == n_a:
                wk = w_k.astype(BF)
                wv = w_v.astype(BF)
                kg = jnp.tile(k_norm_g, nh2).reshape(1, nh2 * HEAD_DIM)
                kp, vp, kp_b, vp_b = _shared_kv(xp, kv_norm_g, wk, wv, seg, kg)
                ks, vs_new, _, _ = _shared_kv(xs, kv_norm_g, wk, wv, seg, kg)
                k_hm = kp_b.reshape(bsz, seq, nh2, HEAD_DIM).transpose(0, 2, 1, 3)
                v_b = vp_b.reshape(bsz, seq, N_HEADS * V_DIM)
            i = l - n_a
            lam_init = 0.8 - 0.6 * math.exp(-0.3 * l)
            wq = w_q[i].astype(BF)
            qg = jnp.tile(q_norm_g[i], nh2).reshape(1, nh2 * HEAD_DIM)
            lams = [v[i].reshape(1, HEAD_DIM) for v in (lambda_q1, lambda_k1, lambda_q2, lambda_k2)]
            sg = subln_g[i].reshape(1, V_DIM)
            wo = w_o[i].astype(BF)
            qp = _q_proj(xp, b_norm_g[i], wq, seg, qg)
            q_hm = qp.reshape(bsz, seq, nh2, HEAD_DIM).transpose(0, 2, 1, 3)
            op = _prompt_attention(q_hm, k_hm, v_b, lams, sg, lam_init, slopes)
            xp = _o_proj(xp, op.reshape(bsz * seq, -1), wo)
            qs = _q_proj(xs, b_norm_g[i], wq, seg, qg)
            os_ = _decode_attention(qs, ks, vs_new, cache_k, cache_v, page_table, lams, sg, lam_init, slopes)
            xs = _o_proj(xs, os_, wo)
        wqt = peer_w_q[l].T.astype(BF)
        keys = peer_keys[l].reshape(PEER_HEADS * 2, N_KEYS, -1).astype(BF)
        u = peer_u[l].astype(BF)
        vt = peer_v[l].reshape(-1, TE_PEER, d).transpose(0, 2, 1).astype(BF)
        xp = _peer(xp, f_norm_g[l], wqt, keys, u, vt)
        xs = _peer(xs, f_norm_g[l], wqt, keys, u, vt)

    state_gmlp_v = jnp.stack(gmlp_v_rows, axis=0)
    return (xp.reshape(bsz, seq, d), xs.reshape(r, dec_seq, d),
            kp.reshape(bsz, seq, nh2, HEAD_DIM), vp.reshape(bsz, seq, N_HEADS, V_DIM),
            ks.reshape(r, dec_seq, nh2, HEAD_DIM), vs_new.reshape(r, dec_seq, N_HEADS, V_DIM),
            state_gmlp_v)
```

```python
import functools
import math

import numpy as np
import jax
import jax.numpy as jnp
from jax import lax
from jax.experimental import pallas as pl
from jax.experimental.pallas import tpu as pltpu

F32 = jnp.float32
BF = jnp.bfloat16

EPS = 1e-6
MASK_VALUE = -1e9
M_INIT = -1e30
LOG2E = math.log2(math.e)

LANES = 128
CHUNK = 128
N_GROUPS = 8
N_HEADS = 8
HEAD_DIM = 64
V_DIM = 128
PAGE = 128
PEER_HEADS = 8
N_KEYS = 128
PEER_TOPK = 16

VMEM_LIMIT = 56 * 1024 * 1024

TM_DENSE = 512
TM_PEER = 512
TE_PEER = 1024
TQ_ATTN = 512
HEAD_MAPS_PER_PASS = 8
PAGES_PER_STEP = 8


def _params(sem):
    return pltpu.CompilerParams(dimension_semantics=sem, vmem_limit_bytes=VMEM_LIMIT)


def _gelu(x):
    return 0.5 * x * (1.0 + jnp.tanh(0.7978845608028654 * (x + 0.044715 * (x * x * x))))


def _rms(x, g):
    ms = jnp.mean(x * x, axis=-1, keepdims=True)
    return x * lax.rsqrt(ms + EPS) * g


def _mm(a, b):
    return jnp.dot(a, b, preferred_element_type=F32)


def _mm_nt(a, b):
    return lax.dot_general(a, b, (((1,), (1,)), ((), ())), preferred_element_type=F32)


def _row_spec(tm, d):
    return pl.BlockSpec((tm, d), lambda i: (i, 0))


def _full_spec(shape):
    nd = len(shape)
    return pl.BlockSpec(shape, lambda *_: (0,) * nd)


def _block_rows(n, pref):
    tm = min(pref, n)
    assert n % tm == 0, (n, tm)
    return tm


def _gmlp_front(x_ref, ng_ref, win_ref, vg_ref):
    x = x_ref[...]
    xn = _rms(x, ng_ref[...]).astype(BF)
    z = _gelu(_mm(xn, win_ref[...]))
    gd = z.shape[1] // 2
    return x, z[:, :gd], _rms(z[:, gd:], vg_ref[...])


def _gmlp_prompt_body(x_ref, ng_ref, win_ref, vg_ref, ws_ref, bias_ref, wout_ref, xo_ref, gated_ref):
    x, u, vn = _gmlp_front(x_ref, ng_ref, win_ref, vg_ref)
    vb = vn.astype(BF)
    tm, gd = u.shape
    gw = gd // N_GROUPS
    row = lax.broadcasted_iota(jnp.int32, (CHUNK, CHUNK), 0)
    col = lax.broadcasted_iota(jnp.int32, (CHUNK, CHUNK), 1)
    causal = col <= row
    for g in range(N_GROUPS):
        w = jnp.where(causal, ws_ref[g], 0.0).astype(BF)
        for c in range(tm // CHUNK):
            rs = slice(c * CHUNK, (c + 1) * CHUNK)
            cs = slice(g * gw, (g + 1) * gw)
            mixed = _mm(w, vb[rs, cs]) + bias_ref[:, cs]
            gated_ref[rs, cs] = (u[rs, cs] * mixed).astype(BF)
    xo_ref[...] = x + _mm(gated_ref[...], wout_ref[...])


def _gmlp_sample_body(x_ref, ng_ref, win_ref, vg_ref, w0_ref, b0_ref, wout_ref, xo_ref, vn_ref):
    x, u, vn = _gmlp_front(x_ref, ng_ref, win_ref, vg_ref)
    vn_ref[...] = vn
    gated = (u * (vn * w0_ref[...] + b0_ref[...])).astype(BF)
    xo_ref[...] = x + _mm(gated, wout_ref[...])


def _gmlp_prompt(x, ng, win, vg, ws, bs, wout):
    n, d = x.shape
    gd = wout.shape[0]
    tm = _block_rows(n, TM_DENSE)
    assert tm % CHUNK == 0
    bias = jnp.repeat(bs.T, gd // N_GROUPS, axis=1)
    return pl.pallas_call(
        _gmlp_prompt_body,
        out_shape=jax.ShapeDtypeStruct((n, d), F32),
        grid=(n // tm,),
        in_specs=[_row_spec(tm, d), _full_spec((1, d)), _full_spec(win.shape), _full_spec((1, gd)),
                  _full_spec(ws.shape), _full_spec(bias.shape), _full_spec(wout.shape)],
        out_specs=_row_spec(tm, d),
        scratch_shapes=[pltpu.VMEM((tm, gd), BF)],
        compiler_params=_params(("parallel",)),
        name="gmlp_prompt",
    )(x, ng.reshape(1, d), win, vg.reshape(1, gd), ws, bias, wout)


def _gmlp_sample(x, ng, win, vg, ws, bs, wout):
    n, d = x.shape
    gd = wout.shape[0]
    gw = gd // N_GROUPS
    tm = _block_rows(n, TM_DENSE)
    w0 = jnp.repeat(ws[:, 0, 0], gw).reshape(1, gd)
    b0 = jnp.repeat(bs[:, 0], gw).reshape(1, gd)
    return pl.pallas_call(
        _gmlp_sample_body,
        out_shape=(jax.ShapeDtypeStruct((n, d), F32), jax.ShapeDtypeStruct((n, gd), F32)),
        grid=(n // tm,),
        in_specs=[_row_spec(tm, d), _full_spec((1, d)), _full_spec(win.shape), _full_spec((1, gd)),
                  _full_spec((1, gd)), _full_spec((1, gd)), _full_spec(wout.shape)],
        out_specs=(_row_spec(tm, d), _row_spec(tm, gd)),
        compiler_params=_params(("parallel",)),
        name="gmlp_sample",
    )(x, ng.reshape(1, d), win, vg.reshape(1, gd), w0, b0, wout)


def _head_norm(y, seg_ref, gt):
    y2 = y * y
    hi = y2.astype(BF)
    lo = (y2 - hi.astype(F32)).astype(BF)
    ms = _mm(hi, seg_ref[...]) + _mm(lo, seg_ref[...])
    return y * lax.rsqrt(ms + EPS) * gt


def _seg_mean_matrix(d, seg):
    idx = np.arange(d) // seg
    return jnp.asarray((idx[:, None] == idx[None, :]).astype(np.float32) / seg, BF)


def _kv_body(x_ref, g_ref, wk_ref, wv_ref, seg_ref, kg_ref, k_ref, v_ref, kb_ref, vb_ref):
    h = _rms(x_ref[...], g_ref[...]).astype(BF)
    k = _head_norm(_mm(h, wk_ref[...]), seg_ref, kg_ref[...])
    v = _mm(h, wv_ref[...])
    k_ref[...] = k
    v_ref[...] = v
    kb_ref[...] = k.astype(BF)
    vb_ref[...] = v.astype(BF)


def _shared_kv(x, g, wk, wv, seg, kg):
    n, d = x.shape
    dk, dv = wk.shape[1], wv.shape[1]
    tm = _block_rows(n, TM_DENSE)
    return pl.pallas_call(
        _kv_body,
        out_shape=(jax.ShapeDtypeStruct((n, dk), F32), jax.ShapeDtypeStruct((n, dv), F32),
                   jax.ShapeDtypeStruct((n, dk), BF), jax.ShapeDtypeStruct((n, dv), BF)),
        grid=(n // tm,),
        in_specs=[_row_spec(tm, d), _full_spec((1, d)), _full_spec(wk.shape), _full_spec(wv.shape),
                  _full_spec(seg.shape), _full_spec((1, dk))],
        out_specs=(_row_spec(tm, dk), _row_spec(tm, dv), _row_spec(tm, dk), _row_spec(tm, dv)),
        compiler_params=_params(("parallel",)),
        name="shared_kv",
    )(x, g.reshape(1, d), wk, wv, seg, kg)


def _q_body(x_ref, g_ref, wq_ref, seg_ref, qg_ref, q_ref):
    h = _rms(x_ref[...], g_ref[...]).astype(BF)
    q = _head_norm(_mm(h, wq_ref[...]), seg_ref, qg_ref[...])
    q_ref[...] = (q * (HEAD_DIM ** -0.5 * LOG2E)).astype(BF)


def _q_proj(x, g, wq, seg, qg):
    n, d = x.shape
    dq = wq.shape[1]
    tm = _block_rows(n, TM_DENSE)
    return pl.pallas_call(
        _q_body,
        out_shape=jax.ShapeDtypeStruct((n, dq), BF),
        grid=(n // tm,),
        in_specs=[_row_spec(tm, d), _full_spec((1, d)), _full_spec(wq.shape), _full_spec(seg.shape),
                  _full_spec((1, dq))],
        out_specs=_row_spec(tm, dq),
        compiler_params=_params(("parallel",)),
        name="q_proj",
    )(x, g.reshape(1, d), wq, seg, qg)


def _oproj_body(x_ref, o_ref, w_ref, xo_ref):
    xo_ref[...] = x_ref[...] + _mm(o_ref[...], w_ref[...])


def _o_proj(x, o, w):
    n, d = x.shape
    tm = _block_rows(n, TM_DENSE)
    return pl.pallas_call(
        _oproj_body,
        out_shape=jax.ShapeDtypeStruct((n, d), F32),
        grid=(n // tm,),
        in_specs=[_row_spec(tm, d), _row_spec(tm, o.shape[1]), _full_spec(w.shape)],
        out_specs=_row_spec(tm, d),
        compiler_params=_params(("parallel",)),
        name="o_proj",
    )(x, o, w)


def _lambda(lq1, lk1, lq2, lk2, lam_init):
    a = jnp.sum(lq1[...] * lk1[...], axis=-1, keepdims=True)
    b = jnp.sum(lq2[...] * lk2[...], axis=-1, keepdims=True)
    return jnp.exp(a) - jnp.exp(b) + lam_init


def _sub_norm(o, sg, lam_init):
    ms = jnp.mean(o * o, axis=-1, keepdims=True)
    return o * lax.rsqrt(ms + EPS) * sg * (1.0 - lam_init)


def _flash_body(tq_ref, tk_ref, q_ref, k_ref, v_ref, lq1, lk1, lq2, lk2, sg_ref, o_ref,
                m_ref, acc_ref, s_ref, p_ref, *, lam_init, slopes, tq):
    t = pl.program_id(1)
    qi = tq_ref[t]
    kj = tk_ref[t]
    nh2 = q_ref.shape[0]
    group = s_ref.shape[0]

    @pl.when(kj == 0)
    def _():
        m_ref[...] = jnp.full(m_ref.shape, M_INIT, F32)
        acc_ref[...] = jnp.zeros(acc_ref.shape, F32)

    def step(diagonal):
        kpos = (lax.broadcasted_iota(jnp.int32, (1, tq), 1) + (kj - qi) * tq).astype(F32)
        if diagonal:
            row = lax.broadcasted_iota(jnp.int32, (tq, tq), 0)
            col = lax.broadcasted_iota(jnp.int32, (tq, tq), 1)
            allowed = col <= row
        ones = jnp.ones((tq, V_DIM), BF)
        heads_per_pass = group // 2
        for h0 in range(0, N_HEADS, heads_per_pass):
            mhs = [m * N_HEADS + h for h in range(h0, h0 + heads_per_pass) for m in range(2)]
            for idx, mh in enumerate(mhs):
                s_ref[idx] = _mm_nt(q_ref[mh], k_ref[mh])
            for idx, mh in enumerate(mhs):
                s = s_ref[idx] + slopes[mh % N_HEADS] * kpos
                if diagonal:
                    s = jnp.where(allowed, s, MASK_VALUE)
                m_old = m_ref[mh]
                m_new = jnp.maximum(m_old, jnp.max(s, axis=1, keepdims=True))
                alpha = jnp.exp2(m_old - m_new)
                p_ref[idx] = jnp.exp2(s - jnp.tile(m_new, (1, tq // LANES))).astype(BF)
                m_ref[mh] = m_new
                acc_ref[mh] = jnp.tile(alpha, (1, 2)) * acc_ref[mh]
            for idx, mh in enumerate(mhs):
                h = mh % N_HEADS
                v_ext = jnp.concatenate([v_ref[:, h * V_DIM:(h + 1) * V_DIM], ones], axis=1)
                acc_ref[mh] += _mm(p_ref[idx], v_ext)

    @pl.when(kj < qi)
    def _():
        step(False)

    @pl.when(kj == qi)
    def _():
        step(True)
        lam = _lambda(lq1, lk1, lq2, lk2, lam_init)
        for h in range(N_HEADS):
            o0 = acc_ref[h, :, :V_DIM] / acc_ref[h, :, V_DIM:]
            o1 = acc_ref[N_HEADS + h, :, :V_DIM] / acc_ref[N_HEADS + h, :, V_DIM:]
            o = _sub_norm(o0 - lam * o1, sg_ref[...], lam_init)
            o_ref[:, h * V_DIM:(h + 1) * V_DIM] = o.astype(BF)


def _prompt_attention(q_hm, k_hm, v, lams, sg, lam_init, slopes):
    b, nh2, t, hd = q_hm.shape
    dv = v.shape[2]
    tq = _block_rows(t, TQ_ATTN)
    nq = t // tq
    tri_q = np.asarray([qi for qi in range(nq) for _ in range(qi + 1)], np.int32)
    tri_k = np.asarray([kj for qi in range(nq) for kj in range(qi + 1)], np.int32)

    def const_spec(shape):
        return pl.BlockSpec(shape, lambda bi, ti, tqr, tkr: (0,) * len(shape))

    body = functools.partial(_flash_body, lam_init=lam_init, slopes=slopes, tq=tq)
    grid_spec = pltpu.PrefetchScalarGridSpec(
        num_scalar_prefetch=2,
        grid=(b, len(tri_q)),
        in_specs=[pl.BlockSpec((None, nh2, tq, hd), lambda bi, ti, tqr, tkr: (bi, 0, tqr[ti], 0)),
                  pl.BlockSpec((None, nh2, tq, hd), lambda bi, ti, tqr, tkr: (bi, 0, tkr[ti], 0)),
                  pl.BlockSpec((None, tq, dv), lambda bi, ti, tqr, tkr: (bi, tkr[ti], 0)),
                  *[const_spec((1, hd))] * 4, const_spec((1, V_DIM))],
        out_specs=pl.BlockSpec((None, tq, dv), lambda bi, ti, tqr, tkr: (bi, tqr[ti], 0)),
        scratch_shapes=[pltpu.VMEM((nh2, tq, LANES), F32), pltpu.VMEM((nh2, tq, 2 * V_DIM), F32),
                        pltpu.VMEM((HEAD_MAPS_PER_PASS, tq, tq), F32),
                        pltpu.VMEM((HEAD_MAPS_PER_PASS, tq, tq), BF)],
    )
    return pl.pallas_call(
        body,
        out_shape=jax.ShapeDtypeStruct((b, t, dv), BF),
        grid_spec=grid_spec,
        compiler_params=_params(("parallel", "arbitrary")),
        name="prompt_attention",
    )(jnp.asarray(tri_q), jnp.asarray(tri_k), q_hm, k_hm, v, *lams, sg)


def _decode_body(pt_ref, q_ref, kn_ref, vn_ref, slope_ref, expand_ref, *rest, lam_init, past_len, pps):
    kt_refs = rest[:pps]
    vp_refs = rest[pps:2 * pps]
    lq1, lk1, lq2, lk2, sg_ref, o_ref, qb_ref, m_ref, l_ref, acc_ref = rest[2 * pps:]
    g = pl.program_id(1)
    nh2, hd = q_ref.shape

    @pl.when(g == 0)
    def _():
        eye = (lax.broadcasted_iota(jnp.int32, (hd, hd), 0) == lax.broadcasted_iota(jnp.int32, (hd, hd), 1))
        ones = jnp.ones((hd, LANES), BF)
        q = q_ref[...].astype(F32)
        for mh in range(nh2):
            qb_ref[mh] = _mm(jnp.where(eye, q[mh:mh + 1, :], 0.0).astype(BF), ones)
        s_self = jnp.sum(q * kn_ref[...], axis=1, keepdims=True)
        m_ref[...] = jnp.broadcast_to(s_self, m_ref.shape)
        l_ref[...] = jnp.ones(l_ref.shape, F32)
        acc_ref[...] = jnp.concatenate([vn_ref[...], vn_ref[...]], axis=0)

    row = lax.broadcasted_iota(jnp.int32, (nh2, PAGE * N_HEADS), 0)
    col = lax.broadcasted_iota(jnp.int32, (nh2, PAGE * N_HEADS), 1)
    own_head = row % N_HEADS == col % N_HEADS
    for jj in range(pps):
        first_key = (g * pps + jj) * PAGE - past_len
        kpos = (lax.broadcasted_iota(jnp.int32, (1, PAGE), 1) + first_key).astype(F32)
        s = jnp.sum(kt_refs[jj][...] * qb_ref[...], axis=1) + slope_ref[...] * kpos
        m_old = m_ref[...]
        m_new = jnp.maximum(m_old, jnp.max(s, axis=1, keepdims=True))
        alpha = jnp.exp2(m_old - m_new)
        p = jnp.exp2(s - m_new)
        l_ref[...] = alpha * l_ref[...] + jnp.sum(p, axis=1, keepdims=True)
        m_ref[...] = m_new
        p_rows = _mm(p.astype(BF), expand_ref[...])
        p_rows = jnp.where(own_head, p_rows, 0.0).astype(BF)
        v_rows = vp_refs[jj][...].reshape(PAGE * N_HEADS, V_DIM).astype(BF)
        acc_ref[...] = alpha * acc_ref[...] + _mm(p_rows, v_rows)

    @pl.when(g == pl.num_programs(1) - 1)
    def _():
        lam = _lambda(lq1, lk1, lq2, lk2, lam_init)
        o0 = acc_ref[:N_HEADS, :] / l_ref[:N_HEADS, :]
        o1 = acc_ref[N_HEADS:, :] / l_ref[N_HEADS:, :]
        o_ref[...] = _sub_norm(o0 - lam * o1, sg_ref[...], lam_init).astype(BF)


def _decode_attention(q, k_new, v_new, cache_k, cache_v, page_table, lams, sg, lam_init, slopes):
    r, d = q.shape
    n_pages = page_table.shape[1]
    pps = min(PAGES_PER_STEP, n_pages)
    assert n_pages % pps == 0 and cache_k.shape[1] == PAGE
    nh2 = 2 * N_HEADS
    slope_arr = jnp.asarray(np.tile(np.asarray(slopes, np.float32), 2)[:, None] * np.ones((1, LANES), np.float32))
    expand = jnp.asarray(np.repeat(np.eye(PAGE, dtype=np.float32), N_HEADS, axis=1), BF)
    cache_kt = jnp.transpose(cache_k, (0, 2, 3, 1))

    def k_spec(jj):
        return pl.BlockSpec((None, nh2, HEAD_DIM, PAGE), lambda ri, gi, pt: (pt[ri, gi * pps + jj], 0, 0, 0))

    def v_spec(jj):
        return pl.BlockSpec((None, PAGE, N_HEADS, V_DIM), lambda ri, gi, pt: (pt[ri, gi * pps + jj], 0, 0, 0))

    def tok_spec(rows, cols):
        return pl.BlockSpec((None, rows, cols), lambda ri, gi, pt: (ri, 0, 0))

    def const_spec(shape):
        return pl.BlockSpec(shape, lambda ri, gi, pt: (0,) * len(shape))

    body = functools.partial(_decode_body, lam_init=lam_init, past_len=n_pages * PAGE, pps=pps)
    grid_spec = pltpu.PrefetchScalarGridSpec(
        num_scalar_prefetch=1,
        grid=(r, n_pages // pps),
        in_specs=[tok_spec(nh2, HEAD_DIM), tok_spec(nh2, HEAD_DIM), tok_spec(N_HEADS, V_DIM),
                  const_spec((nh2, LANES)), const_spec(expand.shape),
                  *[k_spec(jj) for jj in range(pps)], *[v_spec(jj) for jj in range(pps)],
                  *[const_spec((1, HEAD_DIM))] * 4, const_spec((1, V_DIM))],
        out_specs=tok_spec(N_HEADS, V_DIM),
        scratch_shapes=[pltpu.VMEM((nh2, HEAD_DIM, LANES), F32),
                        pltpu.VMEM((nh2, LANES), F32), pltpu.VMEM((nh2, LANES), F32),
                        pltpu.VMEM((nh2, V_DIM), F32)],
    )
    out = pl.pallas_call(
        body,
        out_shape=jax.ShapeDtypeStruct((r, N_HEADS, V_DIM), BF),
        grid_spec=grid_spec,
        compiler_params=_params(("parallel", "arbitrary")),
        name="decode_attention",
    )(page_table, q.reshape(r, nh2, HEAD_DIM), k_new.reshape(r, nh2, HEAD_DIM),
      v_new.reshape(r, N_HEADS, V_DIM), slope_arr, expand, *([cache_kt] * pps), *([cache_v] * pps), *lams, sg)
    return out.reshape(r, d)


def _top_rank(s, k, order=None):
    if order is None:
        order = lax.broadcasted_iota(jnp.int32, s.shape, 0).astype(F32)
    work = s
    rank = jnp.full(s.shape, float(k), F32)
    vals = []
    for a in range(k):
        m = jnp.max(work, axis=0, keepdims=True)
        first = jnp.min(jnp.where(work == m, order, jnp.inf), axis=0, keepdims=True)
        hit = order == first
        rank = jnp.where(hit, float(a), rank)
        work = jnp.where(hit, -jnp.inf, work)
        vals.append(m)
    return vals, rank


def _top_values(s, k):
    work = s
    vals = []
    for _ in range(k):
        m = jnp.max(work, axis=0, keepdims=True)
        work = jnp.where(work == m, -jnp.inf, work)
        vals.append(m)
    taken = jnp.sum(jnp.where(work != s, 1.0, 0.0), axis=0, keepdims=True)
    return vals, taken == float(k)


KQ = 4
SELECT_UNROLL = 2


def _candidates(v1, v2, k):
    assert KQ * KQ >= k
    v1_all = jnp.concatenate(v1, axis=0)
    v2_all = jnp.concatenate(v2, axis=0)
    ar = lax.broadcasted_iota(jnp.int32, v1_all.shape, 0)
    low_a = jnp.where(ar >= KQ, v1_all, -jnp.inf)
    return jnp.concatenate([v1[a] + v2_all for a in range(KQ)] + [low_a + v2[b] for b in range(KQ)], axis=0)


def _partner_counts(picked, k):
    count_hi = picked[KQ * k:(KQ + 1) * k]
    for b in range(1, KQ):
        count_hi = count_hi + picked[(KQ + b) * k:(KQ + b + 1) * k]
    return [jnp.sum(picked[a * k:(a + 1) * k], axis=0, keepdims=True) if a < KQ else count_hi[a:a + 1]
            for a in range(k)]


def _store_gate_tables(h, ts, a_ref, n_ref, e2_ref, r2_ref, s1, s2, max1, max2, top_sums, n, r2):
    z = jnp.ones_like(top_sums[0])
    for t in range(1, len(top_sums)):
        z = z + jnp.exp(top_sums[t] - top_sums[0])
    a_ref[h, ts] = jnp.exp(s1 - max1) / z
    n_ref[h, ts] = n
    e2_ref[h, ts] = jnp.exp(s2 - max2).astype(e2_ref.dtype)
    r2_ref[h, ts] = r2.astype(r2_ref.dtype)


def _half_scores(h, ts, keys_ref, qt_ref):
    lanes = pl.ds(pl.multiple_of(ts * LANES, LANES), LANES)

    def scores(c):
        hc = 2 * h + c
        qq = qt_ref[pl.ds(pl.multiple_of(hc * N_KEYS, N_KEYS), N_KEYS), lanes]
        return _mm(keys_ref[hc], qq)

    return scores(0), scores(1)


def _peer_select(h, ts, keys_ref, qt_ref, a_ref, n_ref, e2_ref, r2_ref):
    k = PEER_TOPK
    s1, s2 = _half_scores(h, ts, keys_ref, qt_ref)
    v1, distinct1 = _top_values(s1, k)
    v2, distinct2 = _top_values(s2, k)
    cand = _candidates(v1, v2, k)
    vc, distinct_c = _top_values(cand, k)
    counts = _partner_counts(jnp.where(cand >= vc[k - 1], 1.0, 0.0), k)
    n = jnp.zeros_like(s1)
    r2 = jnp.full(s2.shape, float(k), F32)
    for a in range(k):
        n = jnp.where(s1 == v1[a], counts[a], n)
        r2 = jnp.where(s2 == v2[a], float(a), r2)
    _store_gate_tables(h, ts, a_ref, n_ref, e2_ref, r2_ref, s1, s2, v1[0], v2[0], vc, n, r2)
    return jnp.min(jnp.where(distinct1 & distinct2 & distinct_c, 1.0, 0.0)) > 0.5


def _peer_select_ranked(h, ts, keys_ref, qt_ref, a_ref, n_ref, e2_ref, r2_ref):
    k = PEER_TOPK
    s1, s2 = _half_scores(h, ts, keys_ref, qt_ref)
    v1, r1 = _top_rank(s1, k)
    v2, r2 = _top_rank(s2, k)
    cand = _candidates(v1, v2, k)
    row = lax.broadcasted_iota(jnp.int32, cand.shape, 0)
    rb = row - KQ * k
    order = jnp.where(row < KQ * k, row, (rb % k) * k + rb // k).astype(F32)
    vc, rc = _top_rank(cand, k, order)
    counts = _partner_counts(jnp.where(rc < float(k), 1.0, 0.0), k)
    n = jnp.zeros_like(s1)
    for a in range(k):
        n = jnp.where(r1 == float(a), counts[a], n)
    _store_gate_tables(h, ts, a_ref, n_ref, e2_ref, r2_ref, s1, s2, v1[0], v2[0], vc, n, r2)


def _peer_body(x_ref, g_ref, wqt_ref, keys_ref, u_ref, vt_ref, o_ref,
               ht_ref, qt_ref, acc_ref, act0_ref, act1_ref, gate0_ref, gate1_ref, a_ref, n_ref, e2_ref, r2_ref,
               *, select_unroll, select_repeat):
    s = pl.program_id(1)
    n_blocks = pl.num_programs(1) - 2
    tm = x_ref.shape[0]
    te = u_ref.shape[0]
    nts = tm // LANES

    @pl.when(s == 0)
    def _():
        hn = _rms(x_ref[...], g_ref[...])
        ht = hn.T.astype(BF)
        ht_ref[...] = ht
        qt_ref[...] = _mm(wqt_ref[...], ht).astype(BF)
        acc_ref[...] = jnp.zeros(acc_ref.shape, F32)
        act1_ref[...] = jnp.zeros(act1_ref.shape, F32)
        gate0_ref[...] = jnp.zeros(gate0_ref.shape, BF)

        tables = (keys_ref, qt_ref, a_ref, n_ref, e2_ref, r2_ref)

        def select(it, carry):
            tiles = [((it * select_unroll + u) // nts, (it * select_unroll + u) % nts) for u in range(select_unroll)]
            no_ties = [_peer_select(h, ts, *tables) for h, ts in tiles]
            for (h, ts), ok in zip(tiles, no_ties):
                @pl.when(jnp.logical_not(ok))
                def _():
                    _peer_select_ranked(h, ts, *tables)
            return carry

        assert (PEER_HEADS * nts) % select_unroll == 0
        for _ in range(select_repeat):
            lax.fori_loop(0, PEER_HEADS * nts // select_unroll, select, 0)

    gate_block = jnp.clip(s - 1, 0, n_blocks - 1)
    kdepth = 2 * N_KEYS

    def stages(act_cur, act_prev, gate_cur, gate_prev):
        for ii in range(te // N_KEYS):
            i = gate_block * (te // N_KEYS) + ii
            rs = slice(ii * N_KEYS, (ii + 1) * N_KEYS)
            act_cur[rs, :] = _mm(u_ref[rs, :], ht_ref[...])
            for lt in range(nts):
                ls = slice(lt * LANES, (lt + 1) * LANES)
                w = None
                for h in range(PEER_HEADS):
                    gdt = e2_ref.dtype
                    a_row = a_ref[h, lt, pl.ds(i, 1), :].astype(gdt)
                    n_row = n_ref[h, lt, pl.ds(i, 1), :].astype(gdt)
                    t = jnp.where(r2_ref[h, lt] < n_row, e2_ref[h, lt] * a_row, jnp.zeros((), gdt))
                    w = t if w is None else w + t
                gate_cur[rs, ls] = (_gelu(act_prev[rs, ls]).astype(gdt) * w).astype(BF)
            if (ii + 1) * N_KEYS % kdepth == 0:
                ks = slice((ii + 1) * N_KEYS - kdepth, (ii + 1) * N_KEYS)
                acc_ref[...] += _mm(vt_ref[:, ks], gate_prev[ks, :])

    @pl.when(s % 2 == 0)
    def _():
        stages(act0_ref, act1_ref, gate1_ref, gate0_ref)

    @pl.when(s % 2 == 1)
    def _():
        stages(act1_ref, act0_ref, gate0_ref, gate1_ref)

    @pl.when(s == pl.num_programs(1) - 1)
    def _():
        o_ref[...] = x_ref[...] + acc_ref[...].T


def _peer(x, g, wqt, keys, u, vt, gate_dtype=BF, select_unroll=SELECT_UNROLL, select_repeat=1):
    n, d = x.shape
    n_exp = u.shape[0]
    tm = _block_rows(n, TM_PEER)
    te = vt.shape[2]
    assert tm % LANES == 0 and n_exp % te == 0 and te % (2 * N_KEYS) == 0
    table = pltpu.VMEM((PEER_HEADS, tm // LANES, N_KEYS, LANES), F32)
    table_bf = pltpu.VMEM((PEER_HEADS, tm // LANES, N_KEYS, LANES), gate_dtype)
    act = pltpu.VMEM((te, tm), F32)
    gate = pltpu.VMEM((te, tm), BF)
    nb = n_exp // te
    return pl.pallas_call(
        functools.partial(_peer_body, select_unroll=select_unroll, select_repeat=select_repeat),
        out_shape=jax.ShapeDtypeStruct((n, d), F32),
        grid=(n // tm, nb + 2),
        in_specs=[pl.BlockSpec((tm, d), lambda i, j: (i, 0)),
                  pl.BlockSpec((1, d), lambda i, j: (0, 0)),
                  pl.BlockSpec(wqt.shape, lambda i, j: (0, 0)),
                  pl.BlockSpec(keys.shape, lambda i, j: (0, 0, 0)),
                  pl.BlockSpec((te, d), lambda i, j: (jnp.minimum(j, nb - 1), 0)),
                  pl.BlockSpec((None, d, te), lambda i, j: (jnp.clip(j - 2, 0, nb - 1), 0, 0))],
        out_specs=pl.BlockSpec((tm, d), lambda i, j: (i, 0)),
        scratch_shapes=[pltpu.VMEM((d, tm), BF), pltpu.VMEM((wqt.shape[0], tm), BF),
                        pltpu.VMEM((d, tm), F32), act, act, gate, gate,
                        table, table, table_bf, table_bf],
        compiler_params=_params(("parallel", "arbitrary")),
        name="peer",
    )(x, g.reshape(1, d), wqt, keys, u, vt)


def kernel(x_prompt, x_sample, cache_k, cache_v, page_table, a_norm_g, a_w_in, a_vnorm_g, a_w_s, a_b_s, a_w_out,
           kv_norm_g, w_k, w_v, k_norm_g, b_norm_g, w_q, q_norm_g, lambda_q1, lambda_k1, lambda_q2, lambda_k2,
           subln_g, w_o, f_norm_g, peer_w_q, peer_keys, peer_u, peer_v):
    bsz, seq, d = x_prompt.shape
    r, dec_seq, _ = x_sample.shape
    assert dec_seq == 1 and seq % CHUNK == 0
    depth = f_norm_g.shape[0]
    n_a = a_norm_g.shape[0]
    nh2 = 2 * N_HEADS
    slopes = tuple(float(v) * LOG2E for v in 2.0 ** (-8.0 * np.arange(1, N_HEADS + 1, dtype=np.float32) / N_HEADS))

    xp = x_prompt.reshape(bsz * seq, d)
    xs = x_sample.reshape(r, d)
    seg = _seg_mean_matrix(nh2 * HEAD_DIM, HEAD_DIM)
    gmlp_v_rows = []
    for l in range(depth):
        if l < n_a:
            win = a_w_in[l].astype(BF)
            wout = a_w_out[l].astype(BF)
            xp = _gmlp_prompt(xp, a_norm_g[l], win, a_vnorm_g[l], a_w_s[l], a_b_s[l], wout)
            xs, vs = _gmlp_sample(xs, a_norm_g[l], win, a_vnorm_g[l], a_w_s[l], a_b_s[l], wout)
            gmlp_v_rows.append(vs.reshape(r, dec_seq, -1))
        else:
            if l == n_a:
                wk = w_k.astype(BF)
                wv = w_v.astype(BF)
                kg = jnp.tile(k_norm_g, nh2).reshape(1, nh2 * HEAD_DIM)
                kp, vp, kp_b, vp_b = _shared_kv(xp, kv_norm_g, wk, wv, seg, kg)
                ks, vs_new, _, _ = _shared_kv(xs, kv_norm_g, wk, wv, seg, kg)
                k_hm = kp_b.reshape(bsz, seq, nh2, HEAD_DIM).transpose(0, 2, 1, 3)
                v_b = vp_b.reshape(bsz, seq, N_HEADS * V_DIM)
            i = l - n_a
            lam_init = 0.8 - 0.6 * math.exp(-0.3 * l)
            wq = w_q[i].astype(BF)
            qg = jnp.tile(q_norm_g[i], nh2).reshape(1, nh2 * HEAD_DIM)
            lams = [v[i].reshape(1, HEAD_DIM) for v in (lambda_q1, lambda_k1, lambda_q2, lambda_k2)]
            sg = subln_g[i].reshape(1, V_DIM)
            wo = w_o[i].astype(BF)
            qp = _q_proj(xp, b_norm_g[i], wq, seg, qg)
            q_hm = qp.reshape(bsz, seq, nh2, HEAD_DIM).transpose(0, 2, 1, 3)
            op = _prompt_attention(q_hm, k_hm, v_b, lams, sg, lam_init, slopes)
            xp = _o_proj(xp, op.reshape(bsz * seq, -1), wo)
            qs = _q_proj(xs, b_norm_g[i], wq, seg, qg)
            os_ = _decode_attention(qs, ks, vs_new, cache_k, cache_v, page_table, lams, sg, lam_init, slopes)
            xs = _o_proj(xs, os_, wo)
        wqt = peer_w_q[l].T.astype(BF)
        keys = peer_keys[l].reshape(PEER_HEADS * 2, N_KEYS, -1).astype(BF)
        u = peer_u[l].astype(BF)
        vt = peer_v[l].reshape(-1, TE_PEER, d).transpose(0, 2, 1).astype(BF)
        diag = [dict(), dict(gate_dtype=F32), dict(select_repeat=2), dict(select_unroll=1)][l]
        xp = _peer(xp, f_norm_g[l], wqt, keys, u, vt, **diag)
        xs = _peer(xs, f_norm_g[l], wqt, keys, u, vt)

    state_gmlp_v = jnp.stack(gmlp_v_rows, axis=0)
    return (xp.reshape(bsz, seq, d), xs.reshape(r, dec_seq, d),
            kp.reshape(bsz, seq, nh2, HEAD_DIM), vp.reshape(bsz, seq, N_HEADS, V_DIM),
            ks.reshape(r, dec_seq, nh2, HEAD_DIM), vs_new.reshape(r, dec_seq, N_HEADS, V_DIM),
            state_gmlp_v)
```

```python
import functools
import math

import numpy as np
import jax
import jax.numpy as jnp
from jax import lax
from jax.experimental import pallas as pl
from jax.experimental.pallas import tpu as pltpu

F32 = jnp.float32
BF = jnp.bfloat16

EPS = 1e-6
MASK_VALUE = -1e9
M_INIT = -1e30
LOG2E = math.log2(math.e)

LANES = 128
CHUNK = 128
N_GROUPS = 8
N_HEADS = 8
HEAD_DIM = 64
V_DIM = 128
PAGE = 128
PEER_HEADS = 8
N_KEYS = 128
PEER_TOPK = 16

VMEM_LIMIT = 56 * 1024 * 1024

TM_DENSE = 512
TM_PEER = 512
TE_PEER = 512
GATE_ROWS = 1
TQ_ATTN = 512
HEAD_MAPS_PER_PASS = 8
PAGES_PER_STEP = 8


def _params(sem):
    return pltpu.CompilerParams(dimension_semantics=sem, vmem_limit_bytes=VMEM_LIMIT)


def _gelu(x):
    half = 0.5 * x
    return half + half * jnp.tanh(x * (0.7978845608028654 + (0.7978845608028654 * 0.044715) * (x * x)))


def _rms(x, g):
    ms = jnp.mean(x * x, axis=-1, keepdims=True)
    return x * lax.rsqrt(ms + EPS) * g


def _mm(a, b):
    return jnp.dot(a, b, preferred_element_type=F32)


def _row_spec(tm, d):
    return pl.BlockSpec((tm, d), lambda i: (i, 0))


def _full_spec(shape):
    nd = len(shape)
    return pl.BlockSpec(shape, lambda *_: (0,) * nd)


def _block_rows(n, pref):
    tm = min(pref, n)
    assert n % tm == 0, (n, tm)
    return tm


def _gmlp_front(x_ref, ng_ref, win_ref, vg_ref):
    x = x_ref[...]
    xn = _rms(x, ng_ref[...]).astype(BF)
    z = _gelu(_mm(xn, win_ref[...]))
    gd = z.shape[1] // 2
    return x, z[:, :gd], _rms(z[:, gd:], vg_ref[...])


def _gmlp_prompt_body(x_ref, ng_ref, win_ref, vg_ref, ws_ref, bias_ref, wout_ref, xo_ref, gated_ref):
    x, u, vn = _gmlp_front(x_ref, ng_ref, win_ref, vg_ref)
    vb = vn.astype(BF)
    tm, gd = u.shape
    gw = gd // N_GROUPS
    row = lax.broadcasted_iota(jnp.int32, (CHUNK, CHUNK), 0)
    col = lax.broadcasted_iota(jnp.int32, (CHUNK, CHUNK), 1)
    causal = col <= row
    for g in range(N_GROUPS):
        w = jnp.where(causal, ws_ref[g], 0.0).astype(BF)
        for c in range(tm // CHUNK):
            rs = slice(c * CHUNK, (c + 1) * CHUNK)
            cs = slice(g * gw, (g + 1) * gw)
            mixed = _mm(w, vb[rs, cs]) + bias_ref[:, cs]
            gated_ref[rs, cs] = (u[rs, cs] * mixed).astype(BF)
    xo_ref[...] = x + _mm(gated_ref[...], wout_ref[...])


def _gmlp_sample_body(x_ref, ng_ref, win_ref, vg_ref, w0_ref, b0_ref, wout_ref, xo_ref, vn_ref):
    x, u, vn = _gmlp_front(x_ref, ng_ref, win_ref, vg_ref)
    vn_ref[...] = vn
    gated = (u * (vn * w0_ref[...] + b0_ref[...])).astype(BF)
    xo_ref[...] = x + _mm(gated, wout_ref[...])


def _gmlp_prompt(x, ng, win, vg, ws, bs, wout):
    n, d = x.shape
    gd = wout.shape[0]
    tm = _block_rows(n, TM_DENSE)
    assert tm % CHUNK == 0
    bias = jnp.repeat(bs.T, gd // N_GROUPS, axis=1)
    return pl.pallas_call(
        _gmlp_prompt_body,
        out_shape=jax.ShapeDtypeStruct((n, d), F32),
        grid=(n // tm,),
        in_specs=[_row_spec(tm, d), _full_spec((1, d)), _full_spec(win.shape), _full_spec((1, gd)),
                  _full_spec(ws.shape), _full_spec(bias.shape), _full_spec(wout.shape)],
        out_specs=_row_spec(tm, d),
        scratch_shapes=[pltpu.VMEM((tm, gd), BF)],
        compiler_params=_params(("parallel",)),
        name="gmlp_prompt",
    )(x, ng.reshape(1, d), win, vg.reshape(1, gd), ws, bias, wout)


def _gmlp_sample(x, ng, win, vg, ws, bs, wout):
    n, d = x.shape
    gd = wout.shape[0]
    gw = gd // N_GROUPS
    tm = _block_rows(n, TM_DENSE)
    w0 = jnp.repeat(ws[:, 0, 0], gw).reshape(1, gd)
    b0 = jnp.repeat(bs[:, 0], gw).reshape(1, gd)
    return pl.pallas_call(
        _gmlp_sample_body,
        out_shape=(jax.ShapeDtypeStruct((n, d), F32), jax.ShapeDtypeStruct((n, gd), F32)),
        grid=(n // tm,),
        in_specs=[_row_spec(tm, d), _full_spec((1, d)), _full_spec(win.shape), _full_spec((1, gd)),
                  _full_spec((1, gd)), _full_spec((1, gd)), _full_spec(wout.shape)],
        out_specs=(_row_spec(tm, d), _row_spec(tm, gd)),
        compiler_params=_params(("parallel",)),
        name="gmlp_sample",
    )(x, ng.reshape(1, d), win, vg.reshape(1, gd), w0, b0, wout)


def _head_norm(y, seg_ref, gt):
    y2 = y * y
    hi = y2.astype(BF)
    lo = (y2 - hi.astype(F32)).astype(BF)
    ms = _mm(hi, seg_ref[...]) + _mm(lo, seg_ref[...])
    return y * lax.rsqrt(ms + EPS) * gt


def _seg_mean_matrix(d, seg):
    idx = np.arange(d) // seg
    return jnp.asarray((idx[:, None] == idx[None, :]).astype(np.float32) / seg, BF)


def _kv_body(x_ref, g_ref, wk_ref, wv_ref, seg_ref, kg_ref, k_ref, v_ref, kb_ref, vb_ref, *, keys_transposed):
    h = _rms(x_ref[...], g_ref[...]).astype(BF)
    k = _head_norm(_mm(h, wk_ref[...]), seg_ref, kg_ref[...])
    v = _mm(h, wv_ref[...])
    if keys_transposed:
        k = k.T
    k_ref[...] = k
    v_ref[...] = v
    kb_ref[...] = k.astype(BF)
    vb_ref[...] = v.astype(BF)


def _shared_kv(x, g, wk, wv, seg, kg, batch=None):
    n, d = x.shape
    dk, dv = wk.shape[1], wv.shape[1]
    tm = _block_rows(n, TM_DENSE)
    if batch is None:
        k_shape, k_spec = (n, dk), _row_spec(tm, dk)
    else:
        seq = n // batch
        assert seq % tm == 0
        k_shape = (batch, dk, seq)
        k_spec = pl.BlockSpec((None, dk, tm), lambda i: (i // (seq // tm), 0, i % (seq // tm)))
    return pl.pallas_call(
        functools.partial(_kv_body, keys_transposed=batch is not None),
        out_shape=(jax.ShapeDtypeStruct(k_shape, F32), jax.ShapeDtypeStruct((n, dv), F32),
                   jax.ShapeDtypeStruct(k_shape, BF), jax.ShapeDtypeStruct((n, dv), BF)),
        grid=(n // tm,),
        in_specs=[_row_spec(tm, d), _full_spec((1, d)), _full_spec(wk.shape), _full_spec(wv.shape),
                  _full_spec(seg.shape), _full_spec((1, dk))],
        out_specs=(k_spec, _row_spec(tm, dv), k_spec, _row_spec(tm, dv)),
        compiler_params=_params(("parallel",)),
        name="shared_kv",
    )(x, g.reshape(1, d), wk, wv, seg, kg)


def _q_body(x_ref, g_ref, wq_ref, seg_ref, qg_ref, q_ref):
    h = _rms(x_ref[...], g_ref[...]).astype(BF)
    q = _head_norm(_mm(h, wq_ref[...]), seg_ref, qg_ref[...])
    q = (q * (HEAD_DIM ** -0.5 * LOG2E)).astype(BF)
    for mh in range(q_ref.shape[0]):
        q_ref[mh] = q[:, mh * HEAD_DIM:(mh + 1) * HEAD_DIM]


def _q_proj(x, g, wq, seg, qg):
    n, d = x.shape
    dq = wq.shape[1]
    tm = _block_rows(n, TM_DENSE)
    return pl.pallas_call(
        _q_body,
        out_shape=jax.ShapeDtypeStruct((dq // HEAD_DIM, n, HEAD_DIM), BF),
        grid=(n // tm,),
        in_specs=[_row_spec(tm, d), _full_spec((1, d)), _full_spec(wq.shape), _full_spec(seg.shape),
                  _full_spec((1, dq))],
        out_specs=pl.BlockSpec((dq // HEAD_DIM, tm, HEAD_DIM), lambda i: (0, i, 0)),
        compiler_params=_params(("parallel",)),
        name="q_proj",
    )(x, g.reshape(1, d), wq, seg, qg)


def _oproj_body(x_ref, o_ref, w_ref, xo_ref):
    xo_ref[...] = x_ref[...] + _mm(o_ref[...], w_ref[...])


def _o_proj(x, o, w):
    n, d = x.shape
    tm = _block_rows(n, TM_DENSE)
    return pl.pallas_call(
        _oproj_body,
        out_shape=jax.ShapeDtypeStruct((n, d), F32),
        grid=(n // tm,),
        in_specs=[_row_spec(tm, d), _row_spec(tm, o.shape[1]), _full_spec(w.shape)],
        out_specs=_row_spec(tm, d),
        compiler_params=_params(("parallel",)),
        name="o_proj",
    )(x, o, w)


def _lambda(lq1, lk1, lq2, lk2, lam_init):
    a = jnp.sum(lq1[...] * lk1[...], axis=-1, keepdims=True)
    b = jnp.sum(lq2[...] * lk2[...], axis=-1, keepdims=True)
    return jnp.exp(a) - jnp.exp(b) + lam_init


def _sub_norm(o, sg, lam_init):
    ms = jnp.mean(o * o, axis=-1, keepdims=True)
    return o * lax.rsqrt(ms + EPS) * sg * (1.0 - lam_init)


def _flash_body(tq_ref, tk_ref, q_ref, k_ref, v_ref, lq1, lk1, lq2, lk2, sg_ref, o_ref,
                m_ref, acc_ref, s_ref, p_ref, *, lam_init, slopes, tq):
    t = pl.program_id(1)
    qi = tq_ref[t]
    kj = tk_ref[t]
    nh2 = q_ref.shape[0]
    group = s_ref.shape[0]

    @pl.when(kj == 0)
    def _():
        m_ref[...] = jnp.full(m_ref.shape, M_INIT, F32)
        acc_ref[...] = jnp.zeros(acc_ref.shape, F32)

    def step(diagonal):
        kpos = (lax.broadcasted_iota(jnp.int32, (1, tq), 1) + (kj - qi) * tq).astype(F32)
        if diagonal:
            row = lax.broadcasted_iota(jnp.int32, (tq, tq), 0)
            col = lax.broadcasted_iota(jnp.int32, (tq, tq), 1)
            allowed = col <= row
        ones = jnp.ones((tq, V_DIM), BF)
        heads_per_pass = group // 2
        for h0 in range(0, N_HEADS, heads_per_pass):
            mhs = [m * N_HEADS + h for h in range(h0, h0 + heads_per_pass) for m in range(2)]
            for idx, mh in enumerate(mhs):
                s_ref[idx] = _mm(q_ref[mh], k_ref[mh])
            for idx, mh in enumerate(mhs):
                s = s_ref[idx] + slopes[mh % N_HEADS] * kpos
                if diagonal:
                    s = jnp.where(allowed, s, MASK_VALUE)
                m_old = m_ref[mh]
                m_new = jnp.maximum(m_old, jnp.max(s, axis=1, keepdims=True))
                alpha = jnp.exp2(m_old - m_new)
                p_ref[idx] = jnp.exp2(s - jnp.tile(m_new, (1, tq // LANES))).astype(BF)
                m_ref[mh] = m_new
                acc_ref[mh] = jnp.tile(alpha, (1, 2)) * acc_ref[mh]
            for idx, mh in enumerate(mhs):
                h = mh % N_HEADS
                v_ext = jnp.concatenate([v_ref[:, h * V_DIM:(h + 1) * V_DIM], ones], axis=1)
                acc_ref[mh] += _mm(p_ref[idx], v_ext)

    @pl.when(kj < qi)
    def _():
        step(False)

    @pl.when(kj == qi)
    def _():
        step(True)
        lam = _lambda(lq1, lk1, lq2, lk2, lam_init)
        for h in range(N_HEADS):
            o0 = acc_ref[h, :, :V_DIM] / acc_ref[h, :, V_DIM:]
            o1 = acc_ref[N_HEADS + h, :, :V_DIM] / acc_ref[N_HEADS + h, :, V_DIM:]
            o = _sub_norm(o0 - lam * o1, sg_ref[...], lam_init)
            o_ref[:, h * V_DIM:(h + 1) * V_DIM] = o.astype(BF)


def _prompt_attention(q_hm, k_t, v, lams, sg, lam_init, slopes):
    b, nh2, hd, t = k_t.shape
    dv = v.shape[2]
    tq = _block_rows(t, TQ_ATTN)
    nq = t // tq
    tri_q = np.asarray([qi for qi in range(nq) for _ in range(qi + 1)], np.int32)
    tri_k = np.asarray([kj for qi in range(nq) for kj in range(qi + 1)], np.int32)

    def const_spec(shape):
        return pl.BlockSpec(shape, lambda bi, ti, tqr, tkr: (0,) * len(shape))

    body = functools.partial(_flash_body, lam_init=lam_init, slopes=slopes, tq=tq)
    grid_spec = pltpu.PrefetchScalarGridSpec(
        num_scalar_prefetch=2,
        grid=(b, len(tri_q)),
        in_specs=[pl.BlockSpec((nh2, tq, hd), lambda bi, ti, tqr, tkr: (0, bi * nq + tqr[ti], 0)),
                  pl.BlockSpec((None, nh2, hd, tq), lambda bi, ti, tqr, tkr: (bi, 0, 0, tkr[ti])),
                  pl.BlockSpec((None, tq, dv), lambda bi, ti, tqr, tkr: (bi, tkr[ti], 0)),
                  *[const_spec((1, hd))] * 4, const_spec((1, V_DIM))],
        out_specs=pl.BlockSpec((None, tq, dv), lambda bi, ti, tqr, tkr: (bi, tqr[ti], 0)),
        scratch_shapes=[pltpu.VMEM((nh2, tq, LANES), F32), pltpu.VMEM((nh2, tq, 2 * V_DIM), F32),
                        pltpu.VMEM((HEAD_MAPS_PER_PASS, tq, tq), F32),
                        pltpu.VMEM((HEAD_MAPS_PER_PASS, tq, tq), BF)],
    )
    return pl.pallas_call(
        body,
        out_shape=jax.ShapeDtypeStruct((b, t, dv), BF),
        grid_spec=grid_spec,
        compiler_params=_params(("parallel", "arbitrary")),
        name="prompt_attention",
    )(jnp.asarray(tri_q), jnp.asarray(tri_k), q_hm, k_t, v, *lams, sg)


def _decode_body(pt_ref, q_ref, kn_ref, vn_ref, slope_ref, expand_ref, *rest, lam_init, past_len, pps):
    kt_refs = rest[:pps]
    vp_refs = rest[pps:2 * pps]
    lq1, lk1, lq2, lk2, sg_ref, o_ref, qb_ref, m_ref, l_ref, acc_ref = rest[2 * pps:]
    g = pl.program_id(1)
    nh2, hd = q_ref.shape

    @pl.when(g == 0)
    def _():
        eye = (lax.broadcasted_iota(jnp.int32, (hd, hd), 0) == lax.broadcasted_iota(jnp.int32, (hd, hd), 1))
        ones = jnp.ones((hd, LANES), BF)
        q = q_ref[...].astype(F32)
        for mh in range(nh2):
            qb_ref[mh] = _mm(jnp.where(eye, q[mh:mh + 1, :], 0.0).astype(BF), ones)
        s_self = jnp.sum(q * kn_ref[...], axis=1, keepdims=True)
        m_ref[...] = jnp.broadcast_to(s_self, m_ref.shape)
        l_ref[...] = jnp.ones(l_ref.shape, F32)
        acc_ref[...] = jnp.concatenate([vn_ref[...], vn_ref[...]], axis=0)

    row = lax.broadcasted_iota(jnp.int32, (nh2, PAGE * N_HEADS), 0)
    col = lax.broadcasted_iota(jnp.int32, (nh2, PAGE * N_HEADS), 1)
    own_head = row % N_HEADS == col % N_HEADS
    for jj in range(pps):
        first_key = (g * pps + jj) * PAGE - past_len
        kpos = (lax.broadcasted_iota(jnp.int32, (1, PAGE), 1) + first_key).astype(F32)
        s = jnp.sum(kt_refs[jj][...] * qb_ref[...], axis=1) + slope_ref[...] * kpos
        m_old = m_ref[...]
        m_new = jnp.maximum(m_old, jnp.max(s, axis=1, keepdims=True))
        alpha = jnp.exp2(m_old - m_new)
        p = jnp.exp2(s - m_new)
        l_ref[...] = alpha * l_ref[...] + jnp.sum(p, axis=1, keepdims=True)
        m_ref[...] = m_new
        p_rows = _mm(p.astype(BF), expand_ref[...])
        p_rows = jnp.where(own_head, p_rows, 0.0).astype(BF)
        v_rows = vp_refs[jj][...].reshape(PAGE * N_HEADS, V_DIM).astype(BF)
        acc_ref[...] = alpha * acc_ref[...] + _mm(p_rows, v_rows)

    @pl.when(g == pl.num_programs(1) - 1)
    def _():
        lam = _lambda(lq1, lk1, lq2, lk2, lam_init)
        o0 = acc_ref[:N_HEADS, :] / l_ref[:N_HEADS, :]
        o1 = acc_ref[N_HEADS:, :] / l_ref[N_HEADS:, :]
        o_ref[...] = _sub_norm(o0 - lam * o1, sg_ref[...], lam_init).astype(BF)


def _decode_attention(q_hm, k_new, v_new, cache_k, cache_v, page_table, lams, sg, lam_init, slopes):
    q = q_hm.transpose(1, 0, 2)
    r = q.shape[0]
    d = q.shape[1] * q.shape[2]
    n_pages = page_table.shape[1]
    pps = min(PAGES_PER_STEP, n_pages)
    assert n_pages % pps == 0 and cache_k.shape[1] == PAGE
    nh2 = 2 * N_HEADS
    slope_arr = jnp.asarray(np.tile(np.asarray(slopes, np.float32), 2)[:, None] * np.ones((1, LANES), np.float32))
    expand = jnp.asarray(np.repeat(np.eye(PAGE, dtype=np.float32), N_HEADS, axis=1), BF)
    cache_kt = jnp.transpose(cache_k, (0, 2, 3, 1))

    def k_spec(jj):
        return pl.BlockSpec((None, nh2, HEAD_DIM, PAGE), lambda ri, gi, pt: (pt[ri, gi * pps + jj], 0, 0, 0))

    def v_spec(jj):
        return pl.BlockSpec((None, PAGE, N_HEADS, V_DIM), lambda ri, gi, pt: (pt[ri, gi * pps + jj], 0, 0, 0))

    def tok_spec(rows, cols):
        return pl.BlockSpec((None, rows, cols), lambda ri, gi, pt: (ri, 0, 0))

    def const_spec(shape):
        return pl.BlockSpec(shape, lambda ri, gi, pt: (0,) * len(shape))

    body = functools.partial(_decode_body, lam_init=lam_init, past_len=n_pages * PAGE, pps=pps)
    grid_spec = pltpu.PrefetchScalarGridSpec(
        num_scalar_prefetch=1,
        grid=(r, n_pages // pps),
        in_specs=[tok_spec(nh2, HEAD_DIM), tok_spec(nh2, HEAD_DIM), tok_spec(N_HEADS, V_DIM),
                  const_spec((nh2, LANES)), const_spec(expand.shape),
                  *[k_spec(jj) for jj in range(pps)], *[v_spec(jj) for jj in range(pps)],
                  *[const_spec((1, HEAD_DIM))] * 4, const_spec((1, V_DIM))],
        out_specs=tok_spec(N_HEADS, V_DIM),
        scratch_shapes=[pltpu.VMEM((nh2, HEAD_DIM, LANES), F32),
                        pltpu.VMEM((nh2, LANES), F32), pltpu.VMEM((nh2, LANES), F32),
                        pltpu.VMEM((nh2, V_DIM), F32)],
    )
    out = pl.pallas_call(
        body,
        out_shape=jax.ShapeDtypeStruct((r, N_HEADS, V_DIM), BF),
        grid_spec=grid_spec,
        compiler_params=_params(("parallel", "arbitrary")),
        name="decode_attention",
    )(page_table, q, k_new.reshape(r, nh2, HEAD_DIM),
      v_new.reshape(r, N_HEADS, V_DIM), slope_arr, expand, *([cache_kt] * pps), *([cache_v] * pps), *lams, sg)
    return out.reshape(r, d)


def _top_rank(s, k, order=None):
    if order is None:
        order = lax.broadcasted_iota(jnp.int32, s.shape, 0).astype(F32)
    work = s
    rank = jnp.full(s.shape, float(k), F32)
    vals = []
    for a in range(k):
        m = jnp.max(work, axis=0, keepdims=True)
        first = jnp.min(jnp.where(work == m, order, jnp.inf), axis=0, keepdims=True)
        hit = order == first
        rank = jnp.where(hit, float(a), rank)
        work = jnp.where(hit, -jnp.inf, work)
        vals.append(m)
    return vals, rank


def _top_values(s, k):
    work = s
    vals = []
    for _ in range(k):
        m = jnp.max(work, axis=0, keepdims=True)
        work = jnp.where(work == m, -jnp.inf, work)
        vals.append(m)
    taken = jnp.sum(jnp.where(work != s, 1.0, 0.0), axis=0, keepdims=True)
    return vals, taken == float(k)


KQ = 4
SELECT_UNROLL = 2


def _candidates(v1, v2, k):
    assert KQ * KQ >= k
    v1_all = jnp.concatenate(v1, axis=0)
    v2_all = jnp.concatenate(v2, axis=0)
    ar = lax.broadcasted_iota(jnp.int32, v1_all.shape, 0)
    low_a = jnp.where(ar >= KQ, v1_all, -jnp.inf)
    return jnp.concatenate([v1[a] + v2_all for a in range(KQ)] + [low_a + v2[b] for b in range(KQ)], axis=0)


def _partner_counts(picked, k):
    count_hi = picked[KQ * k:(KQ + 1) * k]
    for b in range(1, KQ):
        count_hi = count_hi + picked[(KQ + b) * k:(KQ + b + 1) * k]
    return [jnp.sum(picked[a * k:(a + 1) * k], axis=0, keepdims=True) if a < KQ else count_hi[a:a + 1]
            for a in range(k)]


def _store_gate_tables(h, ts, a_ref, n_ref, e2_ref, r2_ref, s1, s2, max1, max2, top_sums, n, r2):
    z = jnp.ones_like(top_sums[0])
    for t in range(1, len(top_sums)):
        z = z + jnp.exp(top_sums[t] - top_sums[0])
    a_ref[h, ts] = jnp.exp(s1 - max1) / z
    n_ref[h, ts] = n
    e2_ref[h, ts] = jnp.exp(s2 - max2).astype(e2_ref.dtype)
    r2_ref[h, ts] = r2.astype(r2_ref.dtype)


def _half_scores(h, ts, keys_ref, qt_ref):
    lanes = pl.ds(pl.multiple_of(ts * LANES, LANES), LANES)

    def scores(c):
        hc = 2 * h + c
        qq = qt_ref[pl.ds(pl.multiple_of(hc * N_KEYS, N_KEYS), N_KEYS), lanes]
        return _mm(keys_ref[hc], qq)

    return scores(0), scores(1)


def _peer_select(h, ts, keys_ref, qt_ref, a_ref, n_ref, e2_ref, r2_ref):
    k = PEER_TOPK
    s1, s2 = _half_scores(h, ts, keys_ref, qt_ref)
    v1, distinct1 = _top_values(s1, k)
    v2, distinct2 = _top_values(s2, k)
    cand = _candidates(v1, v2, k)
    vc, distinct_c = _top_values(cand, k)
    counts = _partner_counts(jnp.where(cand >= vc[k - 1], 1.0, 0.0), k)
    n = jnp.zeros_like(s1)
    r2 = jnp.full(s2.shape, float(k), F32)
    for a in range(k):
        n = jnp.where(s1 == v1[a], counts[a], n)
        r2 = jnp.where(s2 == v2[a], float(a), r2)
    _store_gate_tables(h, ts, a_ref, n_ref, e2_ref, r2_ref, s1, s2, v1[0], v2[0], vc, n, r2)
    return jnp.min(jnp.where(distinct1 & distinct2 & distinct_c, 1.0, 0.0)) > 0.5


def _peer_select_ranked(h, ts, keys_ref, qt_ref, a_ref, n_ref, e2_ref, r2_ref):
    k = PEER_TOPK
    s1, s2 = _half_scores(h, ts, keys_ref, qt_ref)
    v1, r1 = _top_rank(s1, k)
    v2, r2 = _top_rank(s2, k)
    cand = _candidates(v1, v2, k)
    row = lax.broadcasted_iota(jnp.int32, cand.shape, 0)
    rb = row - KQ * k
    order = jnp.where(row < KQ * k, row, (rb % k) * k + rb // k).astype(F32)
    vc, rc = _top_rank(cand, k, order)
    counts = _partner_counts(jnp.where(rc < float(k), 1.0, 0.0), k)
    n = jnp.zeros_like(s1)
    for a in range(k):
        n = jnp.where(r1 == float(a), counts[a], n)
    _store_gate_tables(h, ts, a_ref, n_ref, e2_ref, r2_ref, s1, s2, v1[0], v2[0], vc, n, r2)


def _peer_body(x_ref, g_ref, wqt_ref, keys_ref, u_ref, vt_ref, o_ref,
               ht_ref, qt_ref, acc_ref, act0_ref, act1_ref, gate0_ref, gate1_ref, a_ref, n_ref, e2_ref, r2_ref):
    s = pl.program_id(1)
    n_blocks = pl.num_programs(1) - 2
    tm = x_ref.shape[0]
    te = u_ref.shape[0]
    nts = tm // LANES

    @pl.when(s == 0)
    def _():
        hn = _rms(x_ref[...], g_ref[...])
        ht = hn.T.astype(BF)
        ht_ref[...] = ht
        qt_ref[...] = _mm(wqt_ref[...], ht).astype(BF)
        acc_ref[...] = jnp.zeros(acc_ref.shape, F32)
        act1_ref[...] = jnp.zeros(act1_ref.shape, F32)
        gate0_ref[...] = jnp.zeros(gate0_ref.shape, BF)

        tables = (keys_ref, qt_ref, a_ref, n_ref, e2_ref, r2_ref)

        def select(it, carry):
            tiles = [((it * SELECT_UNROLL + u) // nts, (it * SELECT_UNROLL + u) % nts) for u in range(SELECT_UNROLL)]
            no_ties = [_peer_select(h, ts, *tables) for h, ts in tiles]
            for (h, ts), ok in zip(tiles, no_ties):
                @pl.when(jnp.logical_not(ok))
                def _():
                    _peer_select_ranked(h, ts, *tables)
            return carry

        assert (PEER_HEADS * nts) % SELECT_UNROLL == 0
        lax.fori_loop(0, PEER_HEADS * nts // SELECT_UNROLL, select, 0)

    gate_block = jnp.clip(s - 1, 0, n_blocks - 1)
    kdepth = 2 * N_KEYS

    def stages(act_cur, act_prev, gate_cur, gate_prev):
        for i0 in range(0, te // N_KEYS, GATE_ROWS):
            chunks = []
            for ii in range(i0, i0 + GATE_ROWS):
                rs = slice(ii * N_KEYS, (ii + 1) * N_KEYS)
                act_cur[rs, :] = _mm(u_ref[rs, :], ht_ref[...])
                chunks.append((gate_block * (te // N_KEYS) + ii, rs))
            for lt in range(nts):
                ls = slice(lt * LANES, (lt + 1) * LANES)
                w = [None] * GATE_ROWS
                for h in range(PEER_HEADS):
                    r2 = r2_ref[h, lt]
                    e2 = e2_ref[h, lt]
                    for c, (i, _) in enumerate(chunks):
                        a_row = a_ref[h, lt, pl.ds(i, 1), :]
                        n_row = n_ref[h, lt, pl.ds(i, 1), :]
                        t = jnp.where(r2 < n_row, e2 * a_row, 0.0)
                        w[c] = t if w[c] is None else w[c] + t
                for c, (_, rs) in enumerate(chunks):
                    gate_cur[rs, ls] = (_gelu(act_prev[rs, ls]) * w[c]).astype(BF)
            hi = (i0 + GATE_ROWS) * N_KEYS
            if hi % kdepth == 0:
                acc_ref[...] += _mm(vt_ref[:, hi - kdepth:hi], gate_prev[hi - kdepth:hi, :])

    @pl.when(s % 2 == 0)
    def _():
        stages(act0_ref, act1_ref, gate1_ref, gate0_ref)

    @pl.when(s % 2 == 1)
    def _():
        stages(act1_ref, act0_ref, gate0_ref, gate1_ref)

    @pl.when(s == pl.num_programs(1) - 1)
    def _():
        o_ref[...] = x_ref[...] + acc_ref[...].T


def _peer(x, g, wqt, keys, u, vt):
    n, d = x.shape
    n_exp = u.shape[0]
    tm = _block_rows(n, TM_PEER)
    te = vt.shape[2]
    assert tm % LANES == 0 and n_exp % te == 0 and te % (2 * N_KEYS) == 0
    table = pltpu.VMEM((PEER_HEADS, tm // LANES, N_KEYS, LANES), F32)
    act = pltpu.VMEM((te, tm), F32)
    gate = pltpu.VMEM((te, tm), BF)
    nb = n_exp // te
    return pl.pallas_call(
        _peer_body,
        out_shape=jax.ShapeDtypeStruct((n, d), F32),
        grid=(n // tm, nb + 2),
        in_specs=[pl.BlockSpec((tm, d), lambda i, j: (i, 0)),
                  pl.BlockSpec((1, d), lambda i, j: (0, 0)),
                  pl.BlockSpec(wqt.shape, lambda i, j: (0, 0)),
                  pl.BlockSpec(keys.shape, lambda i, j: (0, 0, 0)),
                  pl.BlockSpec((te, d), lambda i, j: (jnp.minimum(j, nb - 1), 0)),
                  pl.BlockSpec((None, d, te), lambda i, j: (jnp.clip(j - 2, 0, nb - 1), 0, 0))],
        out_specs=pl.BlockSpec((tm, d), lambda i, j: (i, 0)),
        scratch_shapes=[pltpu.VMEM((d, tm), BF), pltpu.VMEM((wqt.shape[0], tm), BF),
                        pltpu.VMEM((d, tm), F32), act, act, gate, gate,
                        table, table, table, table],
        compiler_params=_params(("parallel", "arbitrary")),
        name="peer",
    )(x, g.reshape(1, d), wqt, keys, u, vt)


def kernel(x_prompt, x_sample, cache_k, cache_v, page_table, a_norm_g, a_w_in, a_vnorm_g, a_w_s, a_b_s, a_w_out,
           kv_norm_g, w_k, w_v, k_norm_g, b_norm_g, w_q, q_norm_g, lambda_q1, lambda_k1, lambda_q2, lambda_k2,
           subln_g, w_o, f_norm_g, peer_w_q, peer_keys, peer_u, peer_v):
    bsz, seq, d = x_prompt.shape
    r, dec_seq, _ = x_sample.shape
    assert dec_seq == 1 and seq % CHUNK == 0
    depth = f_norm_g.shape[0]
    n_a = a_norm_g.shape[0]
    nh2 = 2 * N_HEADS
    slopes = tuple(float(v) * LOG2E for v in 2.0 ** (-8.0 * np.arange(1, N_HEADS + 1, dtype=np.float32) / N_HEADS))

    xp = x_prompt.reshape(bsz * seq, d)
    xs = x_sample.reshape(r, d)
    seg = _seg_mean_matrix(nh2 * HEAD_DIM, HEAD_DIM)
    gmlp_v_rows = []
    for l in range(depth):
        if l < n_a:
            win = a_w_in[l].astype(BF)
            wout = a_w_out[l].astype(BF)
            xp = _gmlp_prompt(xp, a_norm_g[l], win, a_vnorm_g[l], a_w_s[l], a_b_s[l], wout)
            xs, vs = _gmlp_sample(xs, a_norm_g[l], win, a_vnorm_g[l], a_w_s[l], a_b_s[l], wout)
            gmlp_v_rows.append(vs.reshape(r, dec_seq, -1))
        else:
            if l == n_a:
                wk = w_k.astype(BF)
                wv = w_v.astype(BF)
                kg = jnp.tile(k_norm_g, nh2).reshape(1, nh2 * HEAD_DIM)
                kp_t, vp, kp_tb, vp_b = _shared_kv(xp, kv_norm_g, wk, wv, seg, kg, batch=bsz)
                ks, vs_new, _, _ = _shared_kv(xs, kv_norm_g, wk, wv, seg, kg)
                k_t = kp_tb.reshape(bsz, nh2, HEAD_DIM, seq)
                v_b = vp_b.reshape(bsz, seq, N_HEADS * V_DIM)
            i = l - n_a
            lam_init = 0.8 - 0.6 * math.exp(-0.3 * l)
            wq = w_q[i].astype(BF)
            qg = jnp.tile(q_norm_g[i], nh2).reshape(1, nh2 * HEAD_DIM)
            lams = [v[i].reshape(1, HEAD_DIM) for v in (lambda_q1, lambda_k1, lambda_q2, lambda_k2)]
            sg = subln_g[i].reshape(1, V_DIM)
            wo = w_o[i].astype(BF)
            q_hm = _q_proj(xp, b_norm_g[i], wq, seg, qg)
            op = _prompt_attention(q_hm, k_t, v_b, lams, sg, lam_init, slopes)
            xp = _o_proj(xp, op.reshape(bsz * seq, -1), wo)
            qs = _q_proj(xs, b_norm_g[i], wq, seg, qg)
            os_ = _decode_attention(qs, ks, vs_new, cache_k, cache_v, page_table, lams, sg, lam_init, slopes)
            xs = _o_proj(xs, os_, wo)
        wqt = peer_w_q[l].T.astype(BF)
        keys = peer_keys[l].reshape(PEER_HEADS * 2, N_KEYS, -1).astype(BF)
        u = peer_u[l].astype(BF)
        vt = peer_v[l].reshape(-1, TE_PEER, d).transpose(0, 2, 1).astype(BF)
        xp = _peer(xp, f_norm_g[l], wqt, keys, u, vt)
        xs = _peer(xs, f_norm_g[l], wqt, keys, u, vt)

    state_gmlp_v = jnp.stack(gmlp_v_rows, axis=0)
    return (xp.reshape(bsz, seq, d), xs.reshape(r, dec_seq, d),
            kp_t.reshape(bsz, nh2, HEAD_DIM, seq).transpose(0, 3, 1, 2), vp.reshape(bsz, seq, N_HEADS, V_DIM),
            ks.reshape(r, dec_seq, nh2, HEAD_DIM), vs_new.reshape(r, dec_seq, N_HEADS, V_DIM),
            state_gmlp_v)
```

```python
import functools
import math

import numpy as np
import jax
import jax.numpy as jnp
from jax import lax
from jax.experimental import pallas as pl
from jax.experimental.pallas import tpu as pltpu

F32 = jnp.float32
BF = jnp.bfloat16

EPS = 1e-6
MASK_VALUE = -1e9
M_INIT = -1e30
LOG2E = math.log2(math.e)

LANES = 128
CHUNK = 128
N_GROUPS = 8
N_HEADS = 8
HEAD_DIM = 64
V_DIM = 128
PAGE = 128
PEER_HEADS = 8
N_KEYS = 128
PEER_TOPK = 16

VMEM_LIMIT = 56 * 1024 * 1024

TM_DENSE = 512
TM_PEER = 512
TE_PEER = 512
ACT_ROWS = 512
VALUE_DEPTH = 256
VALUE_AFTER = (1, 3)
TQ_ATTN = 512
HEAD_MAPS_PER_PASS = 8
PAGES_PER_STEP = 8


def _params(sem):
    return pltpu.CompilerParams(dimension_semantics=sem, vmem_limit_bytes=VMEM_LIMIT)


def _gelu(x):
    half = 0.5 * x
    return half + half * jnp.tanh(x * (0.7978845608028654 + (0.7978845608028654 * 0.044715) * (x * x)))


def _rms(x, g):
    ms = jnp.mean(x * x, axis=-1, keepdims=True)
    return x * lax.rsqrt(ms + EPS) * g


def _mm(a, b):
    return jnp.dot(a, b, preferred_element_type=F32)


def _row_spec(tm, d):
    return pl.BlockSpec((tm, d), lambda i: (i, 0))


def _full_spec(shape):
    nd = len(shape)
    return pl.BlockSpec(shape, lambda *_: (0,) * nd)


def _block_rows(n, pref):
    tm = min(pref, n)
    assert n % tm == 0, (n, tm)
    return tm


def _gmlp_front(x_ref, ng_ref, win_ref, vg_ref):
    x = x_ref[...]
    xn = _rms(x, ng_ref[...]).astype(BF)
    z = _gelu(_mm(xn, win_ref[...]))
    gd = z.shape[1] // 2
    return x, z[:, :gd], _rms(z[:, gd:], vg_ref[...])


def _gmlp_prompt_body(x_ref, ng_ref, win_ref, vg_ref, ws_ref, bias_ref, wout_ref, xo_ref, gated_ref):
    x, u, vn = _gmlp_front(x_ref, ng_ref, win_ref, vg_ref)
    vb = vn.astype(BF)
    tm, gd = u.shape
    gw = gd // N_GROUPS
    row = lax.broadcasted_iota(jnp.int32, (CHUNK, CHUNK), 0)
    col = lax.broadcasted_iota(jnp.int32, (CHUNK, CHUNK), 1)
    causal = col <= row
    for g in range(N_GROUPS):
        w = jnp.where(causal, ws_ref[g], 0.0).astype(BF)
        for c in range(tm // CHUNK):
            rs = slice(c * CHUNK, (c + 1) * CHUNK)
            cs = slice(g * gw, (g + 1) * gw)
            mixed = _mm(w, vb[rs, cs]) + bias_ref[:, cs]
            gated_ref[rs, cs] = (u[rs, cs] * mixed).astype(BF)
    xo_ref[...] = x + _mm(gated_ref[...], wout_ref[...])


def _gmlp_sample_body(x_ref, ng_ref, win_ref, vg_ref, w0_ref, b0_ref, wout_ref, xo_ref, vn_ref):
    x, u, vn = _gmlp_front(x_ref, ng_ref, win_ref, vg_ref)
    vn_ref[...] = vn
    gated = (u * (vn * w0_ref[...] + b0_ref[...])).astype(BF)
    xo_ref[...] = x + _mm(gated, wout_ref[...])


def _gmlp_prompt(x, ng, win, vg, ws, bs, wout):
    n, d = x.shape
    gd = wout.shape[0]
    tm = _block_rows(n, TM_DENSE)
    assert tm % CHUNK == 0
    bias = jnp.repeat(bs.T, gd // N_GROUPS, axis=1)
    return pl.pallas_call(
        _gmlp_prompt_body,
        out_shape=jax.ShapeDtypeStruct((n, d), F32),
        grid=(n // tm,),
        in_specs=[_row_spec(tm, d), _full_spec((1, d)), _full_spec(win.shape), _full_spec((1, gd)),
                  _full_spec(ws.shape), _full_spec(bias.shape), _full_spec(wout.shape)],
        out_specs=_row_spec(tm, d),
        scratch_shapes=[pltpu.VMEM((tm, gd), BF)],
        compiler_params=_params(("parallel",)),
        name="gmlp_prompt",
    )(x, ng.reshape(1, d), win, vg.reshape(1, gd), ws, bias, wout)


def _gmlp_sample(x, ng, win, vg, ws, bs, wout):
    n, d = x.shape
    gd = wout.shape[0]
    gw = gd // N_GROUPS
    tm = _block_rows(n, TM_DENSE)
    w0 = jnp.repeat(ws[:, 0, 0], gw).reshape(1, gd)
    b0 = jnp.repeat(bs[:, 0], gw).reshape(1, gd)
    return pl.pallas_call(
        _gmlp_sample_body,
        out_shape=(jax.ShapeDtypeStruct((n, d), F32), jax.ShapeDtypeStruct((n, gd), F32)),
        grid=(n // tm,),
        in_specs=[_row_spec(tm, d), _full_spec((1, d)), _full_spec(win.shape), _full_spec((1, gd)),
                  _full_spec((1, gd)), _full_spec((1, gd)), _full_spec(wout.shape)],
        out_specs=(_row_spec(tm, d), _row_spec(tm, gd)),
        compiler_params=_params(("parallel",)),
        name="gmlp_sample",
    )(x, ng.reshape(1, d), win, vg.reshape(1, gd), w0, b0, wout)


def _head_norm(y, seg_ref, gt):
    y2 = y * y
    hi = y2.astype(BF)
    lo = (y2 - hi.astype(F32)).astype(BF)
    ms = _mm(hi, seg_ref[...]) + _mm(lo, seg_ref[...])
    return y * lax.rsqrt(ms + EPS) * gt


def _seg_mean_matrix(d, seg):
    idx = np.arange(d) // seg
    return jnp.asarray((idx[:, None] == idx[None, :]).astype(np.float32) / seg, BF)


def _kv_body(x_ref, g_ref, wk_ref, wv_ref, seg_ref, kg_ref, k_ref, v_ref, kb_ref, vb_ref, *, keys_transposed):
    h = _rms(x_ref[...], g_ref[...]).astype(BF)
    k = _head_norm(_mm(h, wk_ref[...]), seg_ref, kg_ref[...])
    v = _mm(h, wv_ref[...])
    if keys_transposed:
        k = k.T
    k_ref[...] = k
    v_ref[...] = v
    kb_ref[...] = k.astype(BF)
    vb_ref[...] = v.astype(BF)


def _shared_kv(x, g, wk, wv, seg, kg, batch=None):
    n, d = x.shape
    dk, dv = wk.shape[1], wv.shape[1]
    tm = _block_rows(n, TM_DENSE)
    if batch is None:
        k_shape, k_spec = (n, dk), _row_spec(tm, dk)
    else:
        seq = n // batch
        assert seq % tm == 0
        k_shape = (batch, dk, seq)
        k_spec = pl.BlockSpec((None, dk, tm), lambda i: (i // (seq // tm), 0, i % (seq // tm)))
    return pl.pallas_call(
        functools.partial(_kv_body, keys_transposed=batch is not None),
        out_shape=(jax.ShapeDtypeStruct(k_shape, F32), jax.ShapeDtypeStruct((n, dv), F32),
                   jax.ShapeDtypeStruct(k_shape, BF), jax.ShapeDtypeStruct((n, dv), BF)),
        grid=(n // tm,),
        in_specs=[_row_spec(tm, d), _full_spec((1, d)), _full_spec(wk.shape), _full_spec(wv.shape),
                  _full_spec(seg.shape), _full_spec((1, dk))],
        out_specs=(k_spec, _row_spec(tm, dv), k_spec, _row_spec(tm, dv)),
        compiler_params=_params(("parallel",)),
        name="shared_kv",
    )(x, g.reshape(1, d), wk, wv, seg, kg)


def _q_body(x_ref, g_ref, wq_ref, seg_ref, qg_ref, q_ref):
    h = _rms(x_ref[...], g_ref[...]).astype(BF)
    q = _head_norm(_mm(h, wq_ref[...]), seg_ref, qg_ref[...])
    q = (q * (HEAD_DIM ** -0.5 * LOG2E)).astype(BF)
    for mh in range(q_ref.shape[0]):
        q_ref[mh] = q[:, mh * HEAD_DIM:(mh + 1) * HEAD_DIM]


def _q_proj(x, g, wq, seg, qg):
    n, d = x.shape
    dq = wq.shape[1]
    tm = _block_rows(n, TM_DENSE)
    return pl.pallas_call(
        _q_body,
        out_shape=jax.ShapeDtypeStruct((dq // HEAD_DIM, n, HEAD_DIM), BF),
        grid=(n // tm,),
        in_specs=[_row_spec(tm, d), _full_spec((1, d)), _full_spec(wq.shape), _full_spec(seg.shape),
                  _full_spec((1, dq))],
        out_specs=pl.BlockSpec((dq // HEAD_DIM, tm, HEAD_DIM), lambda i: (0, i, 0)),
        compiler_params=_params(("parallel",)),
        name="q_proj",
    )(x, g.reshape(1, d), wq, seg, qg)


def _oproj_body(x_ref, o_ref, w_ref, xo_ref):
    xo_ref[...] = x_ref[...] + _mm(o_ref[...], w_ref[...])


def _o_proj(x, o, w):
    n, d = x.shape
    tm = _block_rows(n, TM_DENSE)
    return pl.pallas_call(
        _oproj_body,
        out_shape=jax.ShapeDtypeStruct((n, d), F32),
        grid=(n // tm,),
        in_specs=[_row_spec(tm, d), _row_spec(tm, o.shape[1]), _full_spec(w.shape)],
        out_specs=_row_spec(tm, d),
        compiler_params=_params(("parallel",)),
        name="o_proj",
    )(x, o, w)


def _lambda(lq1, lk1, lq2, lk2, lam_init):
    a = jnp.sum(lq1[...] * lk1[...], axis=-1, keepdims=True)
    b = jnp.sum(lq2[...] * lk2[...], axis=-1, keepdims=True)
    return jnp.exp(a) - jnp.exp(b) + lam_init


def _sub_norm(o, sg, lam_init):
    ms = jnp.mean(o * o, axis=-1, keepdims=True)
    return o * lax.rsqrt(ms + EPS) * sg * (1.0 - lam_init)


def _flash_body(tq_ref, tk_ref, q_ref, k_ref, v_ref, lq1, lk1, lq2, lk2, sg_ref, o_ref,
                m_ref, acc_ref, s_ref, p_ref, *, lam_init, slopes, tq):
    t = pl.program_id(1)
    qi = tq_ref[t]
    kj = tk_ref[t]
    nh2 = q_ref.shape[0]
    group = s_ref.shape[0]

    @pl.when(kj == 0)
    def _():
        m_ref[...] = jnp.full(m_ref.shape, M_INIT, F32)
        acc_ref[...] = jnp.zeros(acc_ref.shape, F32)

    def step(diagonal):
        kpos = (lax.broadcasted_iota(jnp.int32, (1, tq), 1) + (kj - qi) * tq).astype(F32)
        if diagonal:
            row = lax.broadcasted_iota(jnp.int32, (tq, tq), 0)
            col = lax.broadcasted_iota(jnp.int32, (tq, tq), 1)
            allowed = col <= row
        ones = jnp.ones((tq, V_DIM), BF)
        heads_per_pass = group // 2
        for h0 in range(0, N_HEADS, heads_per_pass):
            mhs = [m * N_HEADS + h for h in range(h0, h0 + heads_per_pass) for m in range(2)]
            for idx, mh in enumerate(mhs):
                s_ref[idx] = _mm(q_ref[mh], k_ref[mh])
            for idx, mh in enumerate(mhs):
                s = s_ref[idx] + slopes[mh % N_HEADS] * kpos
                if diagonal:
                    s = jnp.where(allowed, s, MASK_VALUE)
                m_old = m_ref[mh]
                m_new = jnp.maximum(m_old, jnp.max(s, axis=1, keepdims=True))
                alpha = jnp.exp2(m_old - m_new)
                p_ref[idx] = jnp.exp2(s - jnp.tile(m_new, (1, tq // LANES))).astype(BF)
                m_ref[mh] = m_new
                acc_ref[mh] = jnp.tile(alpha, (1, 2)) * acc_ref[mh]
            for idx, mh in enumerate(mhs):
                h = mh % N_HEADS
                v_ext = jnp.concatenate([v_ref[:, h * V_DIM:(h + 1) * V_DIM], ones], axis=1)
                acc_ref[mh] += _mm(p_ref[idx], v_ext)

    @pl.when(kj < qi)
    def _():
        step(False)

    @pl.when(kj == qi)
    def _():
        step(True)
        lam = _lambda(lq1, lk1, lq2, lk2, lam_init)
        for h in range(N_HEADS):
            o0 = acc_ref[h, :, :V_DIM] / acc_ref[h, :, V_DIM:]
            o1 = acc_ref[N_HEADS + h, :, :V_DIM] / acc_ref[N_HEADS + h, :, V_DIM:]
            o = _sub_norm(o0 - lam * o1, sg_ref[...], lam_init)
            o_ref[:, h * V_DIM:(h + 1) * V_DIM] = o.astype(BF)


def _prompt_attention(q_hm, k_t, v, lams, sg, lam_init, slopes):
    b, nh2, hd, t = k_t.shape
    dv = v.shape[2]
    tq = _block_rows(t, TQ_ATTN)
    nq = t // tq
    tri_q = np.asarray([qi for qi in range(nq) for _ in range(qi + 1)], np.int32)
    tri_k = np.asarray([kj for qi in range(nq) for kj in range(qi + 1)], np.int32)

    def const_spec(shape):
        return pl.BlockSpec(shape, lambda bi, ti, tqr, tkr: (0,) * len(shape))

    body = functools.partial(_flash_body, lam_init=lam_init, slopes=slopes, tq=tq)
    grid_spec = pltpu.PrefetchScalarGridSpec(
        num_scalar_prefetch=2,
        grid=(b, len(tri_q)),
        in_specs=[pl.BlockSpec((nh2, tq, hd), lambda bi, ti, tqr, tkr: (0, bi * nq + tqr[ti], 0)),
                  pl.BlockSpec((None, nh2, hd, tq), lambda bi, ti, tqr, tkr: (bi, 0, 0, tkr[ti])),
                  pl.BlockSpec((None, tq, dv), lambda bi, ti, tqr, tkr: (bi, tkr[ti], 0)),
                  *[const_spec((1, hd))] * 4, const_spec((1, V_DIM))],
        out_specs=pl.BlockSpec((None, tq, dv), lambda bi, ti, tqr, tkr: (bi, tqr[ti], 0)),
        scratch_shapes=[pltpu.VMEM((nh2, tq, LANES), F32), pltpu.VMEM((nh2, tq, 2 * V_DIM), F32),
                        pltpu.VMEM((HEAD_MAPS_PER_PASS, tq, tq), F32),
                        pltpu.VMEM((HEAD_MAPS_PER_PASS, tq, tq), BF)],
    )
    return pl.pallas_call(
        body,
        out_shape=jax.ShapeDtypeStruct((b, t, dv), BF),
        grid_spec=grid_spec,
        compiler_params=_params(("parallel", "arbitrary")),
        name="prompt_attention",
    )(jnp.asarray(tri_q), jnp.asarray(tri_k), q_hm, k_t, v, *lams, sg)


def _decode_body(pt_ref, q_ref, kn_ref, vn_ref, slope_ref, expand_ref, *rest, lam_init, past_len, pps):
    kt_refs = rest[:pps]
    vp_refs = rest[pps:2 * pps]
    lq1, lk1, lq2, lk2, sg_ref, o_ref, qb_ref, m_ref, l_ref, acc_ref = rest[2 * pps:]
    g = pl.program_id(1)
    nh2, hd = q_ref.shape

    @pl.when(g == 0)
    def _():
        eye = (lax.broadcasted_iota(jnp.int32, (hd, hd), 0) == lax.broadcasted_iota(jnp.int32, (hd, hd), 1))
        ones = jnp.ones((hd, LANES), BF)
        q = q_ref[...].astype(F32)
        for mh in range(nh2):
            qb_ref[mh] = _mm(jnp.where(eye, q[mh:mh + 1, :], 0.0).astype(BF), ones)
        s_self = jnp.sum(q * kn_ref[...], axis=1, keepdims=True)
        m_ref[...] = jnp.broadcast_to(s_self, m_ref.shape)
        l_ref[...] = jnp.ones(l_ref.shape, F32)
        acc_ref[...] = jnp.concatenate([vn_ref[...], vn_ref[...]], axis=0)

    row = lax.broadcasted_iota(jnp.int32, (nh2, PAGE * N_HEADS), 0)
    col = lax.broadcasted_iota(jnp.int32, (nh2, PAGE * N_HEADS), 1)
    own_head = row % N_HEADS == col % N_HEADS
    for jj in range(pps):
        first_key = (g * pps + jj) * PAGE - past_len
        kpos = (lax.broadcasted_iota(jnp.int32, (1, PAGE), 1) + first_key).astype(F32)
        s = jnp.sum(kt_refs[jj][...] * qb_ref[...], axis=1) + slope_ref[...] * kpos
        m_old = m_ref[...]
        m_new = jnp.maximum(m_old, jnp.max(s, axis=1, keepdims=True))
        alpha = jnp.exp2(m_old - m_new)
        p = jnp.exp2(s - m_new)
        l_ref[...] = alpha * l_ref[...] + jnp.sum(p, axis=1, keepdims=True)
        m_ref[...] = m_new
        p_rows = _mm(p.astype(BF), expand_ref[...])
        p_rows = jnp.where(own_head, p_rows, 0.0).astype(BF)
        v_rows = vp_refs[jj][...].reshape(PAGE * N_HEADS, V_DIM).astype(BF)
        acc_ref[...] = alpha * acc_ref[...] + _mm(p_rows, v_rows)

    @pl.when(g == pl.num_programs(1) - 1)
    def _():
        lam = _lambda(lq1, lk1, lq2, lk2, lam_init)
        o0 = acc_ref[:N_HEADS, :] / l_ref[:N_HEADS, :]
        o1 = acc_ref[N_HEADS:, :] / l_ref[N_HEADS:, :]
        o_ref[...] = _sub_norm(o0 - lam * o1, sg_ref[...], lam_init).astype(BF)


def _decode_attention(q_hm, k_new, v_new, cache_k, cache_v, page_table, lams, sg, lam_init, slopes):
    q = q_hm.transpose(1, 0, 2)
    r = q.shape[0]
    d = q.shape[1] * q.shape[2]
    n_pages = page_table.shape[1]
    pps = min(PAGES_PER_STEP, n_pages)
    assert n_pages % pps == 0 and cache_k.shape[1] == PAGE
    nh2 = 2 * N_HEADS
    slope_arr = jnp.asarray(np.tile(np.asarray(slopes, np.float32), 2)[:, None] * np.ones((1, LANES), np.float32))
    expand = jnp.asarray(np.repeat(np.eye(PAGE, dtype=np.float32), N_HEADS, axis=1), BF)
    cache_kt = jnp.transpose(cache_k, (0, 2, 3, 1))

    def k_spec(jj):
        return pl.BlockSpec((None, nh2, HEAD_DIM, PAGE), lambda ri, gi, pt: (pt[ri, gi * pps + jj], 0, 0, 0))

    def v_spec(jj):
        return pl.BlockSpec((None, PAGE, N_HEADS, V_DIM), lambda ri, gi, pt: (pt[ri, gi * pps + jj], 0, 0, 0))

    def tok_spec(rows, cols):
        return pl.BlockSpec((None, rows, cols), lambda ri, gi, pt: (ri, 0, 0))

    def const_spec(shape):
        return pl.BlockSpec(shape, lambda ri, gi, pt: (0,) * len(shape))

    body = functools.partial(_decode_body, lam_init=lam_init, past_len=n_pages * PAGE, pps=pps)
    grid_spec = pltpu.PrefetchScalarGridSpec(
        num_scalar_prefetch=1,
        grid=(r, n_pages // pps),
        in_specs=[tok_spec(nh2, HEAD_DIM), tok_spec(nh2, HEAD_DIM), tok_spec(N_HEADS, V_DIM),
                  const_spec((nh2, LANES)), const_spec(expand.shape),
                  *[k_spec(jj) for jj in range(pps)], *[v_spec(jj) for jj in range(pps)],
                  *[const_spec((1, HEAD_DIM))] * 4, const_spec((1, V_DIM))],
        out_specs=tok_spec(N_HEADS, V_DIM),
        scratch_shapes=[pltpu.VMEM((nh2, HEAD_DIM, LANES), F32),
                        pltpu.VMEM((nh2, LANES), F32), pltpu.VMEM((nh2, LANES), F32),
                        pltpu.VMEM((nh2, V_DIM), F32)],
    )
    out = pl.pallas_call(
        body,
        out_shape=jax.ShapeDtypeStruct((r, N_HEADS, V_DIM), BF),
        grid_spec=grid_spec,
        compiler_params=_params(("parallel", "arbitrary")),
        name="decode_attention",
    )(page_table, q, k_new.reshape(r, nh2, HEAD_DIM),
      v_new.reshape(r, N_HEADS, V_DIM), slope_arr, expand, *([cache_kt] * pps), *([cache_v] * pps), *lams, sg)
    return out.reshape(r, d)


def _top_rank(s, k, order=None):
    if order is None:
        order = lax.broadcasted_iota(jnp.int32, s.shape, 0).astype(F32)
    work = s
    rank = jnp.full(s.shape, float(k), F32)
    vals = []
    for a in range(k):
        m = jnp.max(work, axis=0, keepdims=True)
        first = jnp.min(jnp.where(work == m, order, jnp.inf), axis=0, keepdims=True)
        hit = order == first
        rank = jnp.where(hit, float(a), rank)
        work = jnp.where(hit, -jnp.inf, work)
        vals.append(m)
    return vals, rank


def _top_values(s, k):
    work = s
    vals = []
    for _ in range(k):
        m = jnp.max(work, axis=0, keepdims=True)
        work = jnp.where(work == m, -jnp.inf, work)
        vals.append(m)
    taken = jnp.sum(jnp.where(work != s, 1.0, 0.0), axis=0, keepdims=True)
    return vals, taken == float(k)


KQ = 4
SELECT_UNROLL = 2


def _candidates(v1, v2, k):
    assert KQ * KQ >= k
    v1_all = jnp.concatenate(v1, axis=0)
    v2_all = jnp.concatenate(v2, axis=0)
    ar = lax.broadcasted_iota(jnp.int32, v1_all.shape, 0)
    low_a = jnp.where(ar >= KQ, v1_all, -jnp.inf)
    return jnp.concatenate([v1[a] + v2_all for a in range(KQ)] + [low_a + v2[b] for b in range(KQ)], axis=0)


def _partner_counts(picked, k):
    count_hi = picked[KQ * k:(KQ + 1) * k]
    for b in range(1, KQ):
        count_hi = count_hi + picked[(KQ + b) * k:(KQ + b + 1) * k]
    return [jnp.sum(picked[a * k:(a + 1) * k], axis=0, keepdims=True) if a < KQ else count_hi[a:a + 1]
            for a in range(k)]


def _store_gate_tables(h, ts, a_ref, n_ref, e2_ref, r2_ref, s1, s2, max1, max2, top_sums, n, r2):
    z = jnp.ones_like(top_sums[0])
    for t in range(1, len(top_sums)):
        z = z + jnp.exp(top_sums[t] - top_sums[0])
    a_ref[h, ts] = jnp.exp(s1 - max1) / z
    n_ref[h, ts] = n
    e2_ref[h, ts] = jnp.exp(s2 - max2).astype(e2_ref.dtype)
    r2_ref[h, ts] = r2.astype(r2_ref.dtype)


def _half_scores(h, ts, keys_ref, qt_ref):
    lanes = pl.ds(pl.multiple_of(ts * LANES, LANES), LANES)

    def scores(c):
        hc = 2 * h + c
        qq = qt_ref[pl.ds(pl.multiple_of(hc * N_KEYS, N_KEYS), N_KEYS), lanes]
        return _mm(keys_ref[hc], qq)

    return scores(0), scores(1)


def _peer_select(h, ts, keys_ref, qt_ref, a_ref, n_ref, e2_ref, r2_ref):
    k = PEER_TOPK
    s1, s2 = _half_scores(h, ts, keys_ref, qt_ref)
    v1, distinct1 = _top_values(s1, k)
    v2, distinct2 = _top_values(s2, k)
    cand = _candidates(v1, v2, k)
    vc, distinct_c = _top_values(cand, k)
    counts = _partner_counts(jnp.where(cand >= vc[k - 1], 1.0, 0.0), k)
    n = jnp.zeros_like(s1)
    r2 = jnp.full(s2.shape, float(k), F32)
    for a in range(k):
        n = jnp.where(s1 == v1[a], counts[a], n)
        r2 = jnp.where(s2 == v2[a], float(a), r2)
    _store_gate_tables(h, ts, a_ref, n_ref, e2_ref, r2_ref, s1, s2, v1[0], v2[0], vc, n, r2)
    return jnp.min(jnp.where(distinct1 & distinct2 & distinct_c, 1.0, 0.0)) > 0.5


def _peer_select_ranked(h, ts, keys_ref, qt_ref, a_ref, n_ref, e2_ref, r2_ref):
    k = PEER_TOPK
    s1, s2 = _half_scores(h, ts, keys_ref, qt_ref)
    v1, r1 = _top_rank(s1, k)
    v2, r2 = _top_rank(s2, k)
    cand = _candidates(v1, v2, k)
    row = lax.broadcasted_iota(jnp.int32, cand.shape, 0)
    rb = row - KQ * k
    order = jnp.where(row < KQ * k, row, (rb % k) * k + rb // k).astype(F32)
    vc, rc = _top_rank(cand, k, order)
    counts = _partner_counts(jnp.where(rc < float(k), 1.0, 0.0), k)
    n = jnp.zeros_like(s1)
    for a in range(k):
        n = jnp.where(r1 == float(a), counts[a], n)
    _store_gate_tables(h, ts, a_ref, n_ref, e2_ref, r2_ref, s1, s2, v1[0], v2[0], vc, n, r2)


def _peer_body(x_ref, g_ref, wqt_ref, keys_ref, u_ref, vt_ref, o_ref,
               ht_ref, qt_ref, acc_ref, act0_ref, act1_ref, gate0_ref, gate1_ref, a_ref, n_ref, e2_ref, r2_ref):
    s = pl.program_id(1)
    n_blocks = pl.num_programs(1) - 2
    tm = x_ref.shape[0]
    te = u_ref.shape[0]
    nts = tm // LANES

    @pl.when(s == 0)
    def _():
        hn = _rms(x_ref[...], g_ref[...])
        ht = hn.T.astype(BF)
        ht_ref[...] = ht
        qt_ref[...] = _mm(wqt_ref[...], ht).astype(BF)
        acc_ref[...] = jnp.zeros(acc_ref.shape, F32)
        gate1_ref[...] = jnp.zeros(gate1_ref.shape, BF)

        tables = (keys_ref, qt_ref, a_ref, n_ref, e2_ref, r2_ref)

        def select(it, carry):
            tiles = [((it * SELECT_UNROLL + u) // nts, (it * SELECT_UNROLL + u) % nts) for u in range(SELECT_UNROLL)]
            no_ties = [_peer_select(h, ts, *tables) for h, ts in tiles]
            for (h, ts), ok in zip(tiles, no_ties):
                @pl.when(jnp.logical_not(ok))
                def _():
                    _peer_select_ranked(h, ts, *tables)
            return carry

        assert (PEER_HEADS * nts) % SELECT_UNROLL == 0
        lax.fori_loop(0, PEER_HEADS * nts // SELECT_UNROLL, select, 0)

    gate_block = jnp.clip(s - 1, 0, n_blocks - 1)

    def stages(act_cur, act_prev, gate_cur, gate_prev, do_act=True, do_gate=True, do_value=True):
        for ii in range(te // N_KEYS):
            lo, hi = ii * N_KEYS, (ii + 1) * N_KEYS
            if do_act and lo % ACT_ROWS == 0:
                rows = slice(lo, lo + ACT_ROWS)
                act_cur[rows, :] = _mm(u_ref[rows, :], ht_ref[...])
            if do_gate:
                i = gate_block * (te // N_KEYS) + ii
                for lt in range(nts):
                    ls = slice(lt * LANES, (lt + 1) * LANES)
                    w = None
                    for h in range(PEER_HEADS):
                        a_row = a_ref[h, lt, pl.ds(i, 1), :]
                        n_row = n_ref[h, lt, pl.ds(i, 1), :]
                        t = jnp.where(r2_ref[h, lt] < n_row, e2_ref[h, lt] * a_row, 0.0)
                        w = t if w is None else w + t
                    gate_cur[lo:hi, ls] = (_gelu(act_prev[lo:hi, ls]) * w).astype(BF)
            if do_value and ii in VALUE_AFTER:
                c = VALUE_AFTER.index(ii)
                ks = slice(c * VALUE_DEPTH, (c + 1) * VALUE_DEPTH)
                acc_ref[...] += _mm(vt_ref[:, ks], gate_prev[ks, :])

    full = jnp.logical_and(s > 0, s <= n_blocks)

    @pl.when(s == 0)
    def _():
        stages(act0_ref, None, None, None, do_gate=False, do_value=False)

    @pl.when(jnp.logical_and(full, s % 2 == 0))
    def _():
        stages(act0_ref, act1_ref, gate1_ref, gate0_ref)

    @pl.when(jnp.logical_and(full, s % 2 == 1))
    def _():
        stages(act1_ref, act0_ref, gate0_ref, gate1_ref)

    @pl.when(s == n_blocks + 1)
    def _():
        stages(None, None, None, gate1_ref, do_act=False, do_gate=False)

    @pl.when(s == pl.num_programs(1) - 1)
    def _():
        o_ref[...] = x_ref[...] + acc_ref[...].T


def _peer(x, g, wqt, keys, u, vt):
    n, d = x.shape
    n_exp = u.shape[0]
    tm = _block_rows(n, TM_PEER)
    te = vt.shape[2]
    assert tm % LANES == 0 and n_exp % te == 0 and te % (2 * N_KEYS) == 0
    table = pltpu.VMEM((PEER_HEADS, tm // LANES, N_KEYS, LANES), F32)
    act = pltpu.VMEM((te, tm), F32)
    gate = pltpu.VMEM((te, tm), BF)
    nb = n_exp // te
    assert nb % 2 == 0 and te % ACT_ROWS == 0 and te % VALUE_DEPTH == 0
    return pl.pallas_call(
        _peer_body,
        out_shape=jax.ShapeDtypeStruct((n, d), F32),
        grid=(n // tm, nb + 2),
        in_specs=[pl.BlockSpec((tm, d), lambda i, j: (i, 0)),
                  pl.BlockSpec((1, d), lambda i, j: (0, 0)),
                  pl.BlockSpec(wqt.shape, lambda i, j: (0, 0)),
                  pl.BlockSpec(keys.shape, lambda i, j: (0, 0, 0)),
                  pl.BlockSpec((te, d), lambda i, j: (jnp.minimum(j, nb - 1), 0)),
                  pl.BlockSpec((None, d, te), lambda i, j: (jnp.clip(j - 2, 0, nb - 1), 0, 0))],
        out_specs=pl.BlockSpec((tm, d), lambda i, j: (i, 0)),
        scratch_shapes=[pltpu.VMEM((d, tm), BF), pltpu.VMEM((wqt.shape[0], tm), BF),
                        pltpu.VMEM((d, tm), F32), act, act, gate, gate,
                        table, table, table, table],
        compiler_params=_params(("parallel", "arbitrary")),
        name="peer",
    )(x, g.reshape(1, d), wqt, keys, u, vt)


def kernel(x_prompt, x_sample, cache_k, cache_v, page_table, a_norm_g, a_w_in, a_vnorm_g, a_w_s, a_b_s, a_w_out,
           kv_norm_g, w_k, w_v, k_norm_g, b_norm_g, w_q, q_norm_g, lambda_q1, lambda_k1, lambda_q2, lambda_k2,
           subln_g, w_o, f_norm_g, peer_w_q, peer_keys, peer_u, peer_v):
    bsz, seq, d = x_prompt.shape
    r, dec_seq, _ = x_sample.shape
    assert dec_seq == 1 and seq % CHUNK == 0
    depth = f_norm_g.shape[0]
    n_a = a_norm_g.shape[0]
    nh2 = 2 * N_HEADS
    slopes = tuple(float(v) * LOG2E for v in 2.0 ** (-8.0 * np.arange(1, N_HEADS + 1, dtype=np.float32) / N_HEADS))

    xp = x_prompt.reshape(bsz * seq, d)
    xs = x_sample.reshape(r, d)
    seg = _seg_mean_matrix(nh2 * HEAD_DIM, HEAD_DIM)
    gmlp_v_rows = []
    for l in range(depth):
        if l < n_a:
            win = a_w_in[l].astype(BF)
            wout = a_w_out[l].astype(BF)
            xp = _gmlp_prompt(xp, a_norm_g[l], win, a_vnorm_g[l], a_w_s[l], a_b_s[l], wout)
            xs, vs = _gmlp_sample(xs, a_norm_g[l], win, a_vnorm_g[l], a_w_s[l], a_b_s[l], wout)
            gmlp_v_rows.append(vs.reshape(r, dec_seq, -1))
        else:
            if l == n_a:
                wk = w_k.astype(BF)
                wv = w_v.astype(BF)
                kg = jnp.tile(k_norm_g, nh2).reshape(1, nh2 * HEAD_DIM)
                kp_t, vp, kp_tb, vp_b = _shared_kv(xp, kv_norm_g, wk, wv, seg, kg, batch=bsz)
                ks, vs_new, _, _ = _shared_kv(xs, kv_norm_g, wk, wv, seg, kg)
                k_t = kp_tb.reshape(bsz, nh2, HEAD_DIM, seq)
                v_b = vp_b.reshape(bsz, seq, N_HEADS * V_DIM)
            i = l - n_a
            lam_init = 0.8 - 0.6 * math.exp(-0.3 * l)
            wq = w_q[i].astype(BF)
            qg = jnp.tile(q_norm_g[i], nh2).reshape(1, nh2 * HEAD_DIM)
            lams = [v[i].reshape(1, HEAD_DIM) for v in (lambda_q1, lambda_k1, lambda_q2, lambda_k2)]
            sg = subln_g[i].reshape(1, V_DIM)
            wo = w_o[i].astype(BF)
            q_hm = _q_proj(xp, b_norm_g[i], wq, seg, qg)
            op = _prompt_attention(q_hm, k_t, v_b, lams, sg, lam_init, slopes)
            xp = _o_proj(xp, op.reshape(bsz * seq, -1), wo)
            qs = _q_proj(xs, b_norm_g[i], wq, seg, qg)
            os_ = _decode_attention(qs, ks, vs_new, cache_k, cache_v, page_table, lams, sg, lam_init, slopes)
            xs = _o_proj(xs, os_, wo)
        wqt = peer_w_q[l].T.astype(BF)
        keys = peer_keys[l].reshape(PEER_HEADS * 2, N_KEYS, -1).astype(BF)
        u = peer_u[l].astype(BF)
        vt = peer_v[l].reshape(-1, TE_PEER, d).transpose(0, 2, 1).astype(BF)
        xp = _peer(xp, f_norm_g[l], wqt, keys, u, vt)
        xs = _peer(xs, f_norm_g[l], wqt, keys, u, vt)

    state_gmlp_v = jnp.stack(gmlp_v_rows, axis=0)
    return (xp.reshape(bsz, seq, d), xs.reshape(r, dec_seq, d),
            kp_t.reshape(bsz, nh2, HEAD_DIM, seq).transpose(0, 3, 1, 2), vp.reshape(bsz, seq, N_HEADS, V_DIM),
            ks.reshape(r, dec_seq, nh2, HEAD_DIM), vs_new.reshape(r, dec_seq, N_HEADS, V_DIM),
            state_gmlp_v)
```

```python
import functools
import math

import numpy as np
import jax
import jax.numpy as jnp
from jax import lax
from jax.experimental import pallas as pl
from jax.experimental.pallas import tpu as pltpu

F32 = jnp.float32
BF = jnp.bfloat16

EPS = 1e-6
MASK_VALUE = -1e9
M_INIT = -1e30
LOG2E = math.log2(math.e)

LANES = 128
CHUNK = 128
N_GROUPS = 8
N_HEADS = 8
HEAD_DIM = 64
V_DIM = 128
PAGE = 128
PEER_HEADS = 8
N_KEYS = 128
PEER_TOPK = 16

VMEM_LIMIT = 56 * 1024 * 1024

TM_DENSE = 512
TM_PEER = 512
TE_PEER = 512
ACT_ROWS = 512
VALUE_DEPTH = 256
VALUE_AFTER = (1, 3)
TQ_ATTN = 512
HEAD_MAPS_PER_PASS = 8
PAGES_PER_STEP = 16


def _params(sem):
    return pltpu.CompilerParams(dimension_semantics=sem, vmem_limit_bytes=VMEM_LIMIT)


def _gelu(x):
    half = 0.5 * x
    return half + half * jnp.tanh(x * (0.7978845608028654 + (0.7978845608028654 * 0.044715) * (x * x)))


def _rms(x, g):
    ms = jnp.mean(x * x, axis=-1, keepdims=True)
    return x * lax.rsqrt(ms + EPS) * g


def _mm(a, b):
    return jnp.dot(a, b, preferred_element_type=F32)


def _row_spec(tm, d):
    return pl.BlockSpec((tm, d), lambda i: (i, 0))


def _full_spec(shape):
    nd = len(shape)
    return pl.BlockSpec(shape, lambda *_: (0,) * nd)


def _block_rows(n, pref):
    tm = min(pref, n)
    assert n % tm == 0, (n, tm)
    return tm


def _gmlp_front(x_ref, ng_ref, win_ref, vg_ref):
    x = x_ref[...]
    xn = _rms(x, ng_ref[...]).astype(BF)
    z = _gelu(_mm(xn, win_ref[...]))
    gd = z.shape[1] // 2
    return x, z[:, :gd], _rms(z[:, gd:], vg_ref[...])


def _gmlp_prompt_body(x_ref, ng_ref, win_ref, vg_ref, ws_ref, bias_ref, wout_ref, xo_ref, gated_ref):
    x, u, vn = _gmlp_front(x_ref, ng_ref, win_ref, vg_ref)
    vb = vn.astype(BF)
    tm, gd = u.shape
    gw = gd // N_GROUPS
    row = lax.broadcasted_iota(jnp.int32, (CHUNK, CHUNK), 0)
    col = lax.broadcasted_iota(jnp.int32, (CHUNK, CHUNK), 1)
    causal = col <= row
    for g in range(N_GROUPS):
        w = jnp.where(causal, ws_ref[g], 0.0).astype(BF)
        for c in range(tm // CHUNK):
            rs = slice(c * CHUNK, (c + 1) * CHUNK)
            cs = slice(g * gw, (g + 1) * gw)
            mixed = _mm(w, vb[rs, cs]) + bias_ref[:, cs]
            gated_ref[rs, cs] = (u[rs, cs] * mixed).astype(BF)
    xo_ref[...] = x + _mm(gated_ref[...], wout_ref[...])


def _gmlp_sample_body(x_ref, ng_ref, win_ref, vg_ref, w0_ref, b0_ref, wout_ref, xo_ref, vn_ref):
    x, u, vn = _gmlp_front(x_ref, ng_ref, win_ref, vg_ref)
    vn_ref[...] = vn
    gated = (u * (vn * w0_ref[...] + b0_ref[...])).astype(BF)
    xo_ref[...] = x + _mm(gated, wout_ref[...])


def _gmlp_prompt(x, ng, win, vg, ws, bs, wout):
    n, d = x.shape
    gd = wout.shape[0]
    tm = _block_rows(n, TM_DENSE)
    assert tm % CHUNK == 0
    bias = jnp.repeat(bs.T, gd // N_GROUPS, axis=1)
    return pl.pallas_call(
        _gmlp_prompt_body,
        out_shape=jax.ShapeDtypeStruct((n, d), F32),
        grid=(n // tm,),
        in_specs=[_row_spec(tm, d), _full_spec((1, d)), _full_spec(win.shape), _full_spec((1, gd)),
                  _full_spec(ws.shape), _full_spec(bias.shape), _full_spec(wout.shape)],
        out_specs=_row_spec(tm, d),
        scratch_shapes=[pltpu.VMEM((tm, gd), BF)],
        compiler_params=_params(("parallel",)),
        name="gmlp_prompt",
    )(x, ng.reshape(1, d), win, vg.reshape(1, gd), ws, bias, wout)


def _gmlp_sample(x, ng, win, vg, ws, bs, wout):
    n, d = x.shape
    gd = wout.shape[0]
    gw = gd // N_GROUPS
    tm = _block_rows(n, TM_DENSE)
    w0 = jnp.repeat(ws[:, 0, 0], gw).reshape(1, gd)
    b0 = jnp.repeat(bs[:, 0], gw).reshape(1, gd)
    return pl.pallas_call(
        _gmlp_sample_body,
        out_shape=(jax.ShapeDtypeStruct((n, d), F32), jax.ShapeDtypeStruct((n, gd), F32)),
        grid=(n // tm,),
        in_specs=[_row_spec(tm, d), _full_spec((1, d)), _full_spec(win.shape), _full_spec((1, gd)),
                  _full_spec((1, gd)), _full_spec((1, gd)), _full_spec(wout.shape)],
        out_specs=(_row_spec(tm, d), _row_spec(tm, gd)),
        compiler_params=_params(("parallel",)),
        name="gmlp_sample",
    )(x, ng.reshape(1, d), win, vg.reshape(1, gd), w0, b0, wout)


def _head_norm(y, seg_ref, gt):
    y2 = y * y
    hi = y2.astype(BF)
    lo = (y2 - hi.astype(F32)).astype(BF)
    ms = _mm(hi, seg_ref[...]) + _mm(lo, seg_ref[...])
    return y * lax.rsqrt(ms + EPS) * gt


def _seg_mean_matrix(d, seg):
    idx = np.arange(d) // seg
    return jnp.asarray((idx[:, None] == idx[None, :]).astype(np.float32) / seg, BF)


def _kv_body(x_ref, g_ref, wk_ref, wv_ref, seg_ref, kg_ref, k_ref, v_ref, kb_ref, vb_ref, *, keys_transposed):
    h = _rms(x_ref[...], g_ref[...]).astype(BF)
    k = _head_norm(_mm(h, wk_ref[...]), seg_ref, kg_ref[...])
    v = _mm(h, wv_ref[...])
    if keys_transposed:
        k = k.T
    k_ref[...] = k
    v_ref[...] = v
    kb_ref[...] = k.astype(BF)
    vb_ref[...] = v.astype(BF)


def _shared_kv(x, g, wk, wv, seg, kg, batch=None):
    n, d = x.shape
    dk, dv = wk.shape[1], wv.shape[1]
    tm = _block_rows(n, TM_DENSE)
    if batch is None:
        k_shape, k_spec = (n, dk), _row_spec(tm, dk)
    else:
        seq = n // batch
        assert seq % tm == 0
        k_shape = (batch, dk, seq)
        k_spec = pl.BlockSpec((None, dk, tm), lambda i: (i // (seq // tm), 0, i % (seq // tm)))
    return pl.pallas_call(
        functools.partial(_kv_body, keys_transposed=batch is not None),
        out_shape=(jax.ShapeDtypeStruct(k_shape, F32), jax.ShapeDtypeStruct((n, dv), F32),
                   jax.ShapeDtypeStruct(k_shape, BF), jax.ShapeDtypeStruct((n, dv), BF)),
        grid=(n // tm,),
        in_specs=[_row_spec(tm, d), _full_spec((1, d)), _full_spec(wk.shape), _full_spec(wv.shape),
                  _full_spec(seg.shape), _full_spec((1, dk))],
        out_specs=(k_spec, _row_spec(tm, dv), k_spec, _row_spec(tm, dv)),
        compiler_params=_params(("parallel",)),
        name="shared_kv",
    )(x, g.reshape(1, d), wk, wv, seg, kg)


def _q_body(x_ref, g_ref, wq_ref, seg_ref, qg_ref, q_ref):
    h = _rms(x_ref[...], g_ref[...]).astype(BF)
    q = _head_norm(_mm(h, wq_ref[...]), seg_ref, qg_ref[...])
    q = (q * (HEAD_DIM ** -0.5 * LOG2E)).astype(BF)
    for mh in range(q_ref.shape[0]):
        q_ref[mh] = q[:, mh * HEAD_DIM:(mh + 1) * HEAD_DIM]


def _q_proj(x, g, wq, seg, qg):
    n, d = x.shape
    dq = wq.shape[1]
    tm = _block_rows(n, TM_DENSE)
    return pl.pallas_call(
        _q_body,
        out_shape=jax.ShapeDtypeStruct((dq // HEAD_DIM, n, HEAD_DIM), BF),
        grid=(n // tm,),
        in_specs=[_row_spec(tm, d), _full_spec((1, d)), _full_spec(wq.shape), _full_spec(seg.shape),
                  _full_spec((1, dq))],
        out_specs=pl.BlockSpec((dq // HEAD_DIM, tm, HEAD_DIM), lambda i: (0, i, 0)),
        compiler_params=_params(("parallel",)),
        name="q_proj",
    )(x, g.reshape(1, d), wq, seg, qg)


def _oproj_body(x_ref, o_ref, w_ref, xo_ref):
    xo_ref[...] = x_ref[...] + _mm(o_ref[...], w_ref[...])


def _o_proj(x, o, w):
    n, d = x.shape
    tm = _block_rows(n, TM_DENSE)
    return pl.pallas_call(
        _oproj_body,
        out_shape=jax.ShapeDtypeStruct((n, d), F32),
        grid=(n // tm,),
        in_specs=[_row_spec(tm, d), _row_spec(tm, o.shape[1]), _full_spec(w.shape)],
        out_specs=_row_spec(tm, d),
        compiler_params=_params(("parallel",)),
        name="o_proj",
    )(x, o, w)


def _lambda(lq1, lk1, lq2, lk2, lam_init):
    a = jnp.sum(lq1[...] * lk1[...], axis=-1, keepdims=True)
    b = jnp.sum(lq2[...] * lk2[...], axis=-1, keepdims=True)
    return jnp.exp(a) - jnp.exp(b) + lam_init


def _sub_norm(o, sg, lam_init):
    ms = jnp.mean(o * o, axis=-1, keepdims=True)
    return o * lax.rsqrt(ms + EPS) * sg * (1.0 - lam_init)


def _flash_body(tq_ref, tk_ref, q_ref, k_ref, v_ref, lq1, lk1, lq2, lk2, sg_ref, o_ref,
                m_ref, acc_ref, s_ref, p_ref, *, lam_init, slopes, tq):
    t = pl.program_id(1)
    qi = tq_ref[t]
    kj = tk_ref[t]
    nh2 = q_ref.shape[0]
    group = s_ref.shape[0]

    @pl.when(kj == 0)
    def _():
        m_ref[...] = jnp.full(m_ref.shape, M_INIT, F32)
        acc_ref[...] = jnp.zeros(acc_ref.shape, F32)

    def step(diagonal):
        kpos = (lax.broadcasted_iota(jnp.int32, (1, tq), 1) + (kj - qi) * tq).astype(F32)
        if diagonal:
            row = lax.broadcasted_iota(jnp.int32, (tq, tq), 0)
            col = lax.broadcasted_iota(jnp.int32, (tq, tq), 1)
            allowed = col <= row
        ones = jnp.ones((tq, V_DIM), BF)
        heads_per_pass = group // 2
        for h0 in range(0, N_HEADS, heads_per_pass):
            mhs = [m * N_HEADS + h for h in range(h0, h0 + heads_per_pass) for m in range(2)]
            for idx, mh in enumerate(mhs):
                s_ref[idx] = _mm(q_ref[mh], k_ref[mh])
            for idx, mh in enumerate(mhs):
                s = s_ref[idx] + slopes[mh % N_HEADS] * kpos
                if diagonal:
                    s = jnp.where(allowed, s, MASK_VALUE)
                m_old = m_ref[mh]
                m_new = jnp.maximum(m_old, jnp.max(s, axis=1, keepdims=True))
                alpha = jnp.exp2(m_old - m_new)
                p_ref[idx] = jnp.exp2(s - jnp.tile(m_new, (1, tq // LANES))).astype(BF)
                m_ref[mh] = m_new
                acc_ref[mh] = jnp.tile(alpha, (1, 2)) * acc_ref[mh]
            for idx, mh in enumerate(mhs):
                h = mh % N_HEADS
                v_ext = jnp.concatenate([v_ref[:, h * V_DIM:(h + 1) * V_DIM], ones], axis=1)
                acc_ref[mh] += _mm(p_ref[idx], v_ext)

    @pl.when(kj < qi)
    def _():
        step(False)

    @pl.when(kj == qi)
    def _():
        step(True)
        lam = _lambda(lq1, lk1, lq2, lk2, lam_init)
        for h in range(N_HEADS):
            o0 = acc_ref[h, :, :V_DIM] / acc_ref[h, :, V_DIM:]
            o1 = acc_ref[N_HEADS + h, :, :V_DIM] / acc_ref[N_HEADS + h, :, V_DIM:]
            o = _sub_norm(o0 - lam * o1, sg_ref[...], lam_init)
            o_ref[:, h * V_DIM:(h + 1) * V_DIM] = o.astype(BF)


def _prompt_attention(q_hm, k_t, v, lams, sg, lam_init, slopes):
    b, nh2, hd, t = k_t.shape
    dv = v.shape[2]
    tq = _block_rows(t, TQ_ATTN)
    nq = t // tq
    tri_q = np.asarray([qi for qi in range(nq) for _ in range(qi + 1)], np.int32)
    tri_k = np.asarray([kj for qi in range(nq) for kj in range(qi + 1)], np.int32)

    def const_spec(shape):
        return pl.BlockSpec(shape, lambda bi, ti, tqr, tkr: (0,) * len(shape))

    body = functools.partial(_flash_body, lam_init=lam_init, slopes=slopes, tq=tq)
    grid_spec = pltpu.PrefetchScalarGridSpec(
        num_scalar_prefetch=2,
        grid=(b, len(tri_q)),
        in_specs=[pl.BlockSpec((nh2, tq, hd), lambda bi, ti, tqr, tkr: (0, bi * nq + tqr[ti], 0)),
                  pl.BlockSpec((None, nh2, hd, tq), lambda bi, ti, tqr, tkr: (bi, 0, 0, tkr[ti])),
                  pl.BlockSpec((None, tq, dv), lambda bi, ti, tqr, tkr: (bi, tkr[ti], 0)),
                  *[const_spec((1, hd))] * 4, const_spec((1, V_DIM))],
        out_specs=pl.BlockSpec((None, tq, dv), lambda bi, ti, tqr, tkr: (bi, tqr[ti], 0)),
        scratch_shapes=[pltpu.VMEM((nh2, tq, LANES), F32), pltpu.VMEM((nh2, tq, 2 * V_DIM), F32),
                        pltpu.VMEM((HEAD_MAPS_PER_PASS, tq, tq), F32),
                        pltpu.VMEM((HEAD_MAPS_PER_PASS, tq, tq), BF)],
    )
    return pl.pallas_call(
        body,
        out_shape=jax.ShapeDtypeStruct((b, t, dv), BF),
        grid_spec=grid_spec,
        compiler_params=_params(("parallel", "arbitrary")),
        name="prompt_attention",
    )(jnp.asarray(tri_q), jnp.asarray(tri_k), q_hm, k_t, v, *lams, sg)


def _decode_body(pt_ref, q_ref, kn_ref, vn_ref, slope_ref, expand_ref, *rest, lam_init, past_len, pps):
    kt_refs = rest[:pps]
    vp_refs = rest[pps:2 * pps]
    lq1, lk1, lq2, lk2, sg_ref, o_ref, qb_ref, m_ref, l_ref, acc_ref = rest[2 * pps:]
    g = pl.program_id(1)
    nh2, hd = q_ref.shape

    @pl.when(g == 0)
    def _():
        eye = (lax.broadcasted_iota(jnp.int32, (hd, hd), 0) == lax.broadcasted_iota(jnp.int32, (hd, hd), 1))
        ones = jnp.ones((hd, LANES), BF)
        q = q_ref[...].astype(F32)
        for mh in range(nh2):
            qb_ref[mh] = _mm(jnp.where(eye, q[mh:mh + 1, :], 0.0).astype(BF), ones)
        s_self = jnp.sum(q * kn_ref[...], axis=1, keepdims=True)
        m_ref[...] = jnp.broadcast_to(s_self, m_ref.shape)
        l_ref[...] = jnp.ones(l_ref.shape, F32)
        acc_ref[...] = jnp.concatenate([vn_ref[...], vn_ref[...]], axis=0)

    row = lax.broadcasted_iota(jnp.int32, (nh2, PAGE * N_HEADS), 0)
    col = lax.broadcasted_iota(jnp.int32, (nh2, PAGE * N_HEADS), 1)
    own_head = row % N_HEADS == col % N_HEADS
    for jj in range(pps):
        first_key = (g * pps + jj) * PAGE - past_len
        kpos = (lax.broadcasted_iota(jnp.int32, (1, PAGE), 1) + first_key).astype(F32)
        s = jnp.sum(kt_refs[jj][...] * qb_ref[...], axis=1) + slope_ref[...] * kpos
        m_old = m_ref[...]
        m_new = jnp.maximum(m_old, jnp.max(s, axis=1, keepdims=True))
        alpha = jnp.exp2(m_old - m_new)
        p = jnp.exp2(s - m_new)
        l_ref[...] = alpha * l_ref[...] + jnp.sum(p, axis=1, keepdims=True)
        m_ref[...] = m_new
        p_rows = _mm(p.astype(BF), expand_ref[...])
        p_rows = jnp.where(own_head, p_rows, 0.0).astype(BF)
        v_rows = vp_refs[jj][...].reshape(PAGE * N_HEADS, V_DIM).astype(BF)
        acc_ref[...] = alpha * acc_ref[...] + _mm(p_rows, v_rows)

    @pl.when(g == pl.num_programs(1) - 1)
    def _():
        lam = _lambda(lq1, lk1, lq2, lk2, lam_init)
        o0 = acc_ref[:N_HEADS, :] / l_ref[:N_HEADS, :]
        o1 = acc_ref[N_HEADS:, :] / l_ref[N_HEADS:, :]
        o_ref[...] = _sub_norm(o0 - lam * o1, sg_ref[...], lam_init).astype(BF)


def _decode_attention(q_hm, k_new, v_new, cache_k, cache_v, page_table, lams, sg, lam_init, slopes):
    q = q_hm.transpose(1, 0, 2)
    r = q.shape[0]
    d = q.shape[1] * q.shape[2]
    n_pages = page_table.shape[1]
    pps = min(PAGES_PER_STEP, n_pages)
    assert n_pages % pps == 0 and cache_k.shape[1] == PAGE
    nh2 = 2 * N_HEADS
    slope_arr = jnp.asarray(np.tile(np.asarray(slopes, np.float32), 2)[:, None] * np.ones((1, LANES), np.float32))
    expand = jnp.asarray(np.repeat(np.eye(PAGE, dtype=np.float32), N_HEADS, axis=1), BF)
    cache_kt = jnp.transpose(cache_k, (0, 2, 3, 1))

    def k_spec(jj):
        return pl.BlockSpec((None, nh2, HEAD_DIM, PAGE), lambda ri, gi, pt: (pt[ri, gi * pps + jj], 0, 0, 0))

    def v_spec(jj):
        return pl.BlockSpec((None, PAGE, N_HEADS, V_DIM), lambda ri, gi, pt: (pt[ri, gi * pps + jj], 0, 0, 0))

    def tok_spec(rows, cols):
        return pl.BlockSpec((None, rows, cols), lambda ri, gi, pt: (ri, 0, 0))

    def const_spec(shape):
        return pl.BlockSpec(shape, lambda ri, gi, pt: (0,) * len(shape))

    body = functools.partial(_decode_body, lam_init=lam_init, past_len=n_pages * PAGE, pps=pps)
    grid_spec = pltpu.PrefetchScalarGridSpec(
        num_scalar_prefetch=1,
        grid=(r, n_pages // pps),
        in_specs=[tok_spec(nh2, HEAD_DIM), tok_spec(nh2, HEAD_DIM), tok_spec(N_HEADS, V_DIM),
                  const_spec((nh2, LANES)), const_spec(expand.shape),
                  *[k_spec(jj) for jj in range(pps)], *[v_spec(jj) for jj in range(pps)],
                  *[const_spec((1, HEAD_DIM))] * 4, const_spec((1, V_DIM))],
        out_specs=tok_spec(N_HEADS, V_DIM),
        scratch_shapes=[pltpu.VMEM((nh2, HEAD_DIM, LANES), F32),
                        pltpu.VMEM((nh2, LANES), F32), pltpu.VMEM((nh2, LANES), F32),
                        pltpu.VMEM((nh2, V_DIM), F32)],
    )
    out = pl.pallas_call(
        body,
        out_shape=jax.ShapeDtypeStruct((r, N_HEADS, V_DIM), BF),
        grid_spec=grid_spec,
        compiler_params=_params(("parallel", "arbitrary")),
        name="decode_attention",
    )(page_table, q, k_new.reshape(r, nh2, HEAD_DIM),
      v_new.reshape(r, N_HEADS, V_DIM), slope_arr, expand, *([cache_kt] * pps), *([cache_v] * pps), *lams, sg)
    return out.reshape(r, d)


def _top_rank(s, k, order=None):
    if order is None:
        order = lax.broadcasted_iota(jnp.int32, s.shape, 0).astype(F32)
    work = s
    rank = jnp.full(s.shape, float(k), F32)
    vals = []
    for a in range(k):
        m = jnp.max(work, axis=0, keepdims=True)
        first = jnp.min(jnp.where(work == m, order, jnp.inf), axis=0, keepdims=True)
        hit = order == first
        rank = jnp.where(hit, float(a), rank)
        work = jnp.where(hit, -jnp.inf, work)
        vals.append(m)
    return vals, rank


def _top_values(s, k):
    work = s
    vals = []
    for _ in range(k):
        m = jnp.max(work, axis=0, keepdims=True)
        work = jnp.where(work == m, -jnp.inf, work)
        vals.append(m)
    taken = jnp.sum(jnp.where(work != s, 1.0, 0.0), axis=0, keepdims=True)
    return vals, taken == float(k)


KQ = 4
SELECT_UNROLL = 2


def _candidates(v1, v2, k):
    assert KQ * KQ >= k
    v1_all = jnp.concatenate(v1, axis=0)
    v2_all = jnp.concatenate(v2, axis=0)
    ar = lax.broadcasted_iota(jnp.int32, v1_all.shape, 0)
    low_a = jnp.where(ar >= KQ, v1_all, -jnp.inf)
    return jnp.concatenate([v1[a] + v2_all for a in range(KQ)] + [low_a + v2[b] for b in range(KQ)], axis=0)


def _partner_counts(picked, k):
    count_hi = picked[KQ * k:(KQ + 1) * k]
    for b in range(1, KQ):
        count_hi = count_hi + picked[(KQ + b) * k:(KQ + b + 1) * k]
    return [jnp.sum(picked[a * k:(a + 1) * k], axis=0, keepdims=True) if a < KQ else count_hi[a:a + 1]
            for a in range(k)]


def _partner_lows(chosen, v2, k):
    v2_all = jnp.concatenate(v2, axis=0)
    low_hi = jnp.where(chosen[KQ * k:(KQ + 1) * k], v2[0], jnp.inf)
    for b in range(1, KQ):
        low_hi = jnp.minimum(low_hi, jnp.where(chosen[(KQ + b) * k:(KQ + b + 1) * k], v2[b], jnp.inf))
    return [jnp.min(jnp.where(chosen[a * k:(a + 1) * k], v2_all, jnp.inf), axis=0, keepdims=True) if a < KQ
            else low_hi[a:a + 1] for a in range(k)]


def _store_gate_tables(h, ts, a_ref, thr_ref, e2_ref, cmp_ref, s1, s2, max1, max2, top_sums, thr, cmp):
    z = jnp.ones_like(top_sums[0])
    for t in range(1, len(top_sums)):
        z = z + jnp.exp(top_sums[t] - top_sums[0])
    a_ref[h, ts] = jnp.exp(s1 - max1) / z
    thr_ref[h, ts] = thr
    e2_ref[h, ts] = jnp.exp(s2 - max2)
    cmp_ref[h, ts] = cmp


def _half_scores(h, ts, keys_ref, qt_ref):
    lanes = pl.ds(pl.multiple_of(ts * LANES, LANES), LANES)

    def scores(c):
        hc = 2 * h + c
        qq = qt_ref[pl.ds(pl.multiple_of(hc * N_KEYS, N_KEYS), N_KEYS), lanes]
        return _mm(keys_ref[hc], qq)

    return scores(0), scores(1)


def _peer_select(h, ts, keys_ref, qt_ref, a_ref, thr_ref, e2_ref, cmp_ref):
    k = PEER_TOPK
    s1, s2 = _half_scores(h, ts, keys_ref, qt_ref)
    v1, distinct1 = _top_values(s1, k)
    v2, distinct2 = _top_values(s2, k)
    cand = _candidates(v1, v2, k)
    vc, distinct_c = _top_values(cand, k)
    lows = _partner_lows(cand >= vc[k - 1], v2, k)
    thr = jnp.full(s1.shape, jnp.inf, F32)
    for a in range(k):
        thr = jnp.where(s1 == v1[a], lows[a], thr)
    _store_gate_tables(h, ts, a_ref, thr_ref, e2_ref, cmp_ref, s1, s2, v1[0], v2[0], vc, thr, s2)
    return jnp.min(jnp.where(distinct1 & distinct2 & distinct_c, 1.0, 0.0)) > 0.5


def _peer_select_ranked(h, ts, keys_ref, qt_ref, a_ref, thr_ref, e2_ref, cmp_ref):
    k = PEER_TOPK
    s1, s2 = _half_scores(h, ts, keys_ref, qt_ref)
    v1, r1 = _top_rank(s1, k)
    v2, r2 = _top_rank(s2, k)
    cand = _candidates(v1, v2, k)
    row = lax.broadcasted_iota(jnp.int32, cand.shape, 0)
    rb = row - KQ * k
    order = jnp.where(row < KQ * k, row, (rb % k) * k + rb // k).astype(F32)
    vc, rc = _top_rank(cand, k, order)
    counts = _partner_counts(jnp.where(rc < float(k), 1.0, 0.0), k)
    n = jnp.zeros_like(s1)
    for a in range(k):
        n = jnp.where(r1 == float(a), counts[a], n)
    _store_gate_tables(h, ts, a_ref, thr_ref, e2_ref, cmp_ref, s1, s2, v1[0], v2[0], vc, 0.5 - n, -r2)


def _peer_body(x_ref, g_ref, wqt_ref, keys_ref, u_ref, vt_ref, o_ref,
               ht_ref, qt_ref, acc_ref, act0_ref, act1_ref, gate0_ref, gate1_ref, a_ref, thr_ref, e2_ref, cmp_ref):
    s = pl.program_id(1)
    n_blocks = pl.num_programs(1) - 2
    tm = x_ref.shape[0]
    te = u_ref.shape[0]
    nts = tm // LANES

    @pl.when(s == 0)
    def _():
        hn = _rms(x_ref[...], g_ref[...])
        ht = hn.T.astype(BF)
        ht_ref[...] = ht
        qt_ref[...] = _mm(wqt_ref[...], ht).astype(BF)
        acc_ref[...] = jnp.zeros(acc_ref.shape, F32)
        gate1_ref[...] = jnp.zeros(gate1_ref.shape, BF)

        tables = (keys_ref, qt_ref, a_ref, thr_ref, e2_ref, cmp_ref)

        def select(it, carry):
            tiles = [((it * SELECT_UNROLL + u) // nts, (it * SELECT_UNROLL + u) % nts) for u in range(SELECT_UNROLL)]
            no_ties = [_peer_select(h, ts, *tables) for h, ts in tiles]
            for (h, ts), ok in zip(tiles, no_ties):
                @pl.when(jnp.logical_not(ok))
                def _():
                    _peer_select_ranked(h, ts, *tables)
            return carry

        assert (PEER_HEADS * nts) % SELECT_UNROLL == 0
        lax.fori_loop(0, PEER_HEADS * nts // SELECT_UNROLL, select, 0)

    gate_block = jnp.clip(s - 1, 0, n_blocks - 1)

    def stages(act_cur, act_prev, gate_cur, gate_prev, do_act=True, do_gate=True, do_value=True):
        for ii in range(te // N_KEYS):
            lo, hi = ii * N_KEYS, (ii + 1) * N_KEYS
            if do_act and lo % ACT_ROWS == 0:
                rows = slice(lo, lo + ACT_ROWS)
                act_cur[rows, :] = _mm(u_ref[rows, :], ht_ref[...])
            if do_gate:
                i = gate_block * (te // N_KEYS) + ii
                for lt in range(nts):
                    ls = slice(lt * LANES, (lt + 1) * LANES)
                    w = None
                    for h in range(PEER_HEADS):
                        a_row = a_ref[h, lt, pl.ds(i, 1), :]
                        thr_row = thr_ref[h, lt, pl.ds(i, 1), :]
                        t = jnp.where(cmp_ref[h, lt] >= thr_row, e2_ref[h, lt] * a_row, 0.0)
                        w = t if w is None else w + t
                    gate_cur[lo:hi, ls] = (_gelu(act_prev[lo:hi, ls]) * w).astype(BF)
            if do_value and ii in VALUE_AFTER:
                c = VALUE_AFTER.index(ii)
                ks = slice(c * VALUE_DEPTH, (c + 1) * VALUE_DEPTH)
                acc_ref[...] += _mm(vt_ref[:, ks], gate_prev[ks, :])

    full = jnp.logical_and(s > 0, s <= n_blocks)

    @pl.when(s == 0)
    def _():
        stages(act0_ref, None, None, None, do_gate=False, do_value=False)

    @pl.when(jnp.logical_and(full, s % 2 == 0))
    def _():
        stages(act0_ref, act1_ref, gate1_ref, gate0_ref)

    @pl.when(jnp.logical_and(full, s % 2 == 1))
    def _():
        stages(act1_ref, act0_ref, gate0_ref, gate1_ref)

    @pl.when(s == n_blocks + 1)
    def _():
        stages(None, None, None, gate1_ref, do_act=False, do_gate=False)

    @pl.when(s == pl.num_programs(1) - 1)
    def _():
        o_ref[...] = x_ref[...] + acc_ref[...].T


def _peer(x, g, wqt, keys, u, vt):
    n, d = x.shape
    n_exp = u.shape[0]
    tm = _block_rows(n, TM_PEER)
    te = vt.shape[2]
    assert tm % LANES == 0 and n_exp % te == 0 and te % (2 * N_KEYS) == 0
    table = pltpu.VMEM((PEER_HEADS, tm // LANES, N_KEYS, LANES), F32)
    act = pltpu.VMEM((te, tm), F32)
    gate = pltpu.VMEM((te, tm), BF)
    nb = n_exp // te
    assert nb % 2 == 0 and te % ACT_ROWS == 0 and te % VALUE_DEPTH == 0
    return pl.pallas_call(
        _peer_body,
        out_shape=jax.ShapeDtypeStruct((n, d), F32),
        grid=(n // tm, nb + 2),
        in_specs=[pl.BlockSpec((tm, d), lambda i, j: (i, 0)),
                  pl.BlockSpec((1, d), lambda i, j: (0, 0)),
                  pl.BlockSpec(wqt.shape, lambda i, j: (0, 0)),
                  pl.BlockSpec(keys.shape, lambda i, j: (0, 0, 0)),
                  pl.BlockSpec((te, d), lambda i, j: (jnp.minimum(j, nb - 1), 0)),
                  pl.BlockSpec((None, d, te), lambda i, j: (jnp.clip(j - 2, 0, nb - 1), 0, 0))],
        out_specs=pl.BlockSpec((tm, d), lambda i, j: (i, 0)),
        scratch_shapes=[pltpu.VMEM((d, tm), BF), pltpu.VMEM((wqt.shape[0], tm), BF),
                        pltpu.VMEM((d, tm), F32), act, act, gate, gate,
                        table, table, table, table],
        compiler_params=_params(("parallel", "arbitrary")),
        name="peer",
    )(x, g.reshape(1, d), wqt, keys, u, vt)


def kernel(x_prompt, x_sample, cache_k, cache_v, page_table, a_norm_g, a_w_in, a_vnorm_g, a_w_s, a_b_s, a_w_out,
           kv_norm_g, w_k, w_v, k_norm_g, b_norm_g, w_q, q_norm_g, lambda_q1, lambda_k1, lambda_q2, lambda_k2,
           subln_g, w_o, f_norm_g, peer_w_q, peer_keys, peer_u, peer_v):
    bsz, seq, d = x_prompt.shape
    r, dec_seq, _ = x_sample.shape
    assert dec_seq == 1 and seq % CHUNK == 0
    depth = f_norm_g.shape[0]
    n_a = a_norm_g.shape[0]
    nh2 = 2 * N_HEADS
    slopes = tuple(float(v) * LOG2E for v in 2.0 ** (-8.0 * np.arange(1, N_HEADS + 1, dtype=np.float32) / N_HEADS))

    xp = x_prompt.reshape(bsz * seq, d)
    xs = x_sample.reshape(r, d)
    seg = _seg_mean_matrix(nh2 * HEAD_DIM, HEAD_DIM)
    gmlp_v_rows = []
    for l in range(depth):
        if l < n_a:
            win = a_w_in[l].astype(BF)
            wout = a_w_out[l].astype(BF)
            xp = _gmlp_prompt(xp, a_norm_g[l], win, a_vnorm_g[l], a_w_s[l], a_b_s[l], wout)
            xs, vs = _gmlp_sample(xs, a_norm_g[l], win, a_vnorm_g[l], a_w_s[l], a_b_s[l], wout)
            gmlp_v_rows.append(vs.reshape(r, dec_seq, -1))
        else:
            if l == n_a:
                wk = w_k.astype(BF)
                wv = w_v.astype(BF)
                kg = jnp.tile(k_norm_g, nh2).reshape(1, nh2 * HEAD_DIM)
                kp_t, vp, kp_tb, vp_b = _shared_kv(xp, kv_norm_g, wk, wv, seg, kg, batch=bsz)
                ks, vs_new, _, _ = _shared_kv(xs, kv_norm_g, wk, wv, seg, kg)
                k_t = kp_tb.reshape(bsz, nh2, HEAD_DIM, seq)
                v_b = vp_b.reshape(bsz, seq, N_HEADS * V_DIM)
            i = l - n_a
            lam_init = 0.8 - 0.6 * math.exp(-0.3 * l)
            wq = w_q[i].astype(BF)
            qg = jnp.tile(q_norm_g[i], nh2).reshape(1, nh2 * HEAD_DIM)
            lams = [v[i].reshape(1, HEAD_DIM) for v in (lambda_q1, lambda_k1, lambda_q2, lambda_k2)]
            sg = subln_g[i].reshape(1, V_DIM)
            wo = w_o[i].astype(BF)
            q_hm = _q_proj(xp, b_norm_g[i], wq, seg, qg)
            op = _prompt_attention(q_hm, k_t, v_b, lams, sg, lam_init, slopes)
            xp = _o_proj(xp, op.reshape(bsz * seq, -1), wo)
            qs = _q_proj(xs, b_norm_g[i], wq, seg, qg)
            os_ = _decode_attention(qs, ks, vs_new, cache_k, cache_v, page_table, lams, sg, lam_init, slopes)
            xs = _o_proj(xs, os_, wo)
        wqt = peer_w_q[l].T.astype(BF)
        keys = peer_keys[l].reshape(PEER_HEADS * 2, N_KEYS, -1).astype(BF)
        u = peer_u[l].astype(BF)
        vt = peer_v[l].reshape(-1, TE_PEER, d).transpose(0, 2, 1).astype(BF)
        xp = _peer(xp, f_norm_g[l], wqt, keys, u, vt)
        xs = _peer(xs, f_norm_g[l], wqt, keys, u, vt)

    state_gmlp_v = jnp.stack(gmlp_v_rows, axis=0)
    return (xp.reshape(bsz, seq, d), xs.reshape(r, dec_seq, d),
            kp_t.reshape(bsz, nh2, HEAD_DIM, seq).transpose(0, 3, 1, 2), vp.reshape(bsz, seq, N_HEADS, V_DIM),
            ks.reshape(r, dec_seq, nh2, HEAD_DIM), vs_new.reshape(r, dec_seq, N_HEADS, V_DIM),
            state_gmlp_v)
```

```python
import functools
import math

import numpy as np
import jax
import jax.numpy as jnp
from jax import lax
from jax.experimental import pallas as pl
from jax.experimental.pallas import tpu as pltpu

F32 = jnp.float32
BF = jnp.bfloat16

EPS = 1e-6
MASK_VALUE = -1e9
M_INIT = -1e30
LOG2E = math.log2(math.e)

LANES = 128
CHUNK = 128
N_GROUPS = 8
N_HEADS = 8
HEAD_DIM = 64
V_DIM = 128
PAGE = 128
PEER_HEADS = 8
N_KEYS = 128
PEER_TOPK = 16

VMEM_LIMIT = 56 * 1024 * 1024

TM_DENSE = 512
TM_PEER = 512
TE_PEER = 512
ACT_ROWS = 512
VALUE_DEPTH = 256
VALUE_AFTER = (1, 3)
TQ_ATTN = 512
HEAD_MAPS_PER_PASS = 8
PAGES_PER_STEP = 16


def _params(sem):
    return pltpu.CompilerParams(dimension_semantics=sem, vmem_limit_bytes=VMEM_LIMIT)


def _gelu(x):
    half = 0.5 * x
    return half + half * jnp.tanh(x * (0.7978845608028654 + (0.7978845608028654 * 0.044715) * (x * x)))


def _rms(x, g):
    ms = jnp.mean(x * x, axis=-1, keepdims=True)
    return x * lax.rsqrt(ms + EPS) * g


def _mm(a, b):
    return jnp.dot(a, b, preferred_element_type=F32)


def _row_spec(tm, d):
    return pl.BlockSpec((tm, d), lambda i: (i, 0))


def _full_spec(shape):
    nd = len(shape)
    return pl.BlockSpec(shape, lambda *_: (0,) * nd)


def _block_rows(n, pref):
    tm = min(pref, n)
    assert n % tm == 0, (n, tm)
    return tm


def _gmlp_front(x_ref, ng_ref, win_ref, vg_ref):
    x = x_ref[...]
    xn = _rms(x, ng_ref[...]).astype(BF)
    z = _gelu(_mm(xn, win_ref[...]))
    gd = z.shape[1] // 2
    return x, z[:, :gd], _rms(z[:, gd:], vg_ref[...])


def _gmlp_prompt_body(x_ref, ng_ref, win_ref, vg_ref, ws_ref, bias_ref, wout_ref, xo_ref, gated_ref):
    x, u, vn = _gmlp_front(x_ref, ng_ref, win_ref, vg_ref)
    vb = vn.astype(BF)
    tm, gd = u.shape
    gw = gd // N_GROUPS
    row = lax.broadcasted_iota(jnp.int32, (CHUNK, CHUNK), 0)
    col = lax.broadcasted_iota(jnp.int32, (CHUNK, CHUNK), 1)
    causal = col <= row
    for g in range(N_GROUPS):
        w = jnp.where(causal, ws_ref[g], 0.0).astype(BF)
        for c in range(tm // CHUNK):
            rs = slice(c * CHUNK, (c + 1) * CHUNK)
            cs = slice(g * gw, (g + 1) * gw)
            mixed = _mm(w, vb[rs, cs]) + bias_ref[:, cs]
            gated_ref[rs, cs] = (u[rs, cs] * mixed).astype(BF)
    xo_ref[...] = x + _mm(gated_ref[...], wout_ref[...])


def _gmlp_sample_body(x_ref, ng_ref, win_ref, vg_ref, w0_ref, b0_ref, wout_ref, xo_ref, vn_ref):
    x, u, vn = _gmlp_front(x_ref, ng_ref, win_ref, vg_ref)
    vn_ref[...] = vn
    gated = (u * (vn * w0_ref[...] + b0_ref[...])).astype(BF)
    xo_ref[...] = x + _mm(gated, wout_ref[...])


def _gmlp_prompt(x, ng, win, vg, ws, bs, wout):
    n, d = x.shape
    gd = wout.shape[0]
    tm = _block_rows(n, TM_DENSE)
    assert tm % CHUNK == 0
    bias = jnp.repeat(bs.T, gd // N_GROUPS, axis=1)
    return pl.pallas_call(
        _gmlp_prompt_body,
        out_shape=jax.ShapeDtypeStruct((n, d), F32),
        grid=(n // tm,),
        in_specs=[_row_spec(tm, d), _full_spec((1, d)), _full_spec(win.shape), _full_spec((1, gd)),
                  _full_spec(ws.shape), _full_spec(bias.shape), _full_spec(wout.shape)],
        out_specs=_row_spec(tm, d),
        scratch_shapes=[pltpu.VMEM((tm, gd), BF)],
        compiler_params=_params(("parallel",)),
        name="gmlp_prompt",
    )(x, ng.reshape(1, d), win, vg.reshape(1, gd), ws, bias, wout)


def _gmlp_sample(x, ng, win, vg, ws, bs, wout):
    n, d = x.shape
    gd = wout.shape[0]
    gw = gd // N_GROUPS
    tm = _block_rows(n, TM_DENSE)
    w0 = jnp.repeat(ws[:, 0, 0], gw).reshape(1, gd)
    b0 = jnp.repeat(bs[:, 0], gw).reshape(1, gd)
    return pl.pallas_call(
        _gmlp_sample_body,
        out_shape=(jax.ShapeDtypeStruct((n, d), F32), jax.ShapeDtypeStruct((n, gd), F32)),
        grid=(n // tm,),
        in_specs=[_row_spec(tm, d), _full_spec((1, d)), _full_spec(win.shape), _full_spec((1, gd)),
                  _full_spec((1, gd)), _full_spec((1, gd)), _full_spec(wout.shape)],
        out_specs=(_row_spec(tm, d), _row_spec(tm, gd)),
        compiler_params=_params(("parallel",)),
        name="gmlp_sample",
    )(x, ng.reshape(1, d), win, vg.reshape(1, gd), w0, b0, wout)


def _head_norm(y, seg_ref, gt):
    y2 = y * y
    hi = y2.astype(BF)
    lo = (y2 - hi.astype(F32)).astype(BF)
    ms = _mm(hi, seg_ref[...]) + _mm(lo, seg_ref[...])
    return y * lax.rsqrt(ms + EPS) * gt


def _seg_mean_matrix(d, seg):
    idx = np.arange(d) // seg
    return jnp.asarray((idx[:, None] == idx[None, :]).astype(np.float32) / seg, BF)


def _kv_body(x_ref, g_ref, wk_ref, wv_ref, seg_ref, kg_ref, k_ref, v_ref, kb_ref, vb_ref, *, keys_transposed):
    h = _rms(x_ref[...], g_ref[...]).astype(BF)
    k = _head_norm(_mm(h, wk_ref[...]), seg_ref, kg_ref[...])
    v = _mm(h, wv_ref[...])
    if keys_transposed:
        k = k.T
    k_ref[...] = k
    v_ref[...] = v
    kb_ref[...] = k.astype(BF)
    vb_ref[...] = v.astype(BF)


def _shared_kv(x, g, wk, wv, seg, kg, batch=None):
    n, d = x.shape
    dk, dv = wk.shape[1], wv.shape[1]
    tm = _block_rows(n, TM_DENSE)
    if batch is None:
        k_shape, k_spec = (n, dk), _row_spec(tm, dk)
    else:
        seq = n // batch
        assert seq % tm == 0
        k_shape = (batch, dk, seq)
        k_spec = pl.BlockSpec((None, dk, tm), lambda i: (i // (seq // tm), 0, i % (seq // tm)))
    return pl.pallas_call(
        functools.partial(_kv_body, keys_transposed=batch is not None),
        out_shape=(jax.ShapeDtypeStruct(k_shape, F32), jax.ShapeDtypeStruct((n, dv), F32),
                   jax.ShapeDtypeStruct(k_shape, BF), jax.ShapeDtypeStruct((n, dv), BF)),
        grid=(n // tm,),
        in_specs=[_row_spec(tm, d), _full_spec((1, d)), _full_spec(wk.shape), _full_spec(wv.shape),
                  _full_spec(seg.shape), _full_spec((1, dk))],
        out_specs=(k_spec, _row_spec(tm, dv), k_spec, _row_spec(tm, dv)),
        compiler_params=_params(("parallel",)),
        name="shared_kv",
    )(x, g.reshape(1, d), wk, wv, seg, kg)


def _q_body(x_ref, g_ref, wq_ref, seg_ref, qg_ref, q_ref):
    h = _rms(x_ref[...], g_ref[...]).astype(BF)
    q = _head_norm(_mm(h, wq_ref[...]), seg_ref, qg_ref[...])
    q = (q * (HEAD_DIM ** -0.5 * LOG2E)).astype(BF)
    for mh in range(q_ref.shape[0]):
        q_ref[mh] = q[:, mh * HEAD_DIM:(mh + 1) * HEAD_DIM]


def _q_proj(x, g, wq, seg, qg):
    n, d = x.shape
    dq = wq.shape[1]
    tm = _block_rows(n, TM_DENSE)
    return pl.pallas_call(
        _q_body,
        out_shape=jax.ShapeDtypeStruct((dq // HEAD_DIM, n, HEAD_DIM), BF),
        grid=(n // tm,),
        in_specs=[_row_spec(tm, d), _full_spec((1, d)), _full_spec(wq.shape), _full_spec(seg.shape),
                  _full_spec((1, dq))],
        out_specs=pl.BlockSpec((dq // HEAD_DIM, tm, HEAD_DIM), lambda i: (0, i, 0)),
        compiler_params=_params(("parallel",)),
        name="q_proj",
    )(x, g.reshape(1, d), wq, seg, qg)


def _oproj_body(x_ref, o_ref, w_ref, xo_ref):
    xo_ref[...] = x_ref[...] + _mm(o_ref[...], w_ref[...])


def _o_proj(x, o, w):
    n, d = x.shape
    tm = _block_rows(n, TM_DENSE)
    return pl.pallas_call(
        _oproj_body,
        out_shape=jax.ShapeDtypeStruct((n, d), F32),
        grid=(n // tm,),
        in_specs=[_row_spec(tm, d), _row_spec(tm, o.shape[1]), _full_spec(w.shape)],
        out_specs=_row_spec(tm, d),
        compiler_params=_params(("parallel",)),
        name="o_proj",
    )(x, o, w)


def _lambda(lq1, lk1, lq2, lk2, lam_init):
    a = jnp.sum(lq1[...] * lk1[...], axis=-1, keepdims=True)
    b = jnp.sum(lq2[...] * lk2[...], axis=-1, keepdims=True)
    return jnp.exp(a) - jnp.exp(b) + lam_init


def _sub_norm(o, sg, lam_init):
    ms = jnp.mean(o * o, axis=-1, keepdims=True)
    return o * lax.rsqrt(ms + EPS) * sg * (1.0 - lam_init)


def _flash_body(tq_ref, tk_ref, q_ref, k_ref, v_ref, lq1, lk1, lq2, lk2, sg_ref, o_ref,
                m_ref, acc_ref, s_ref, p_ref, *, lam_init, slopes, tq):
    t = pl.program_id(1)
    qi = tq_ref[t]
    kj = tk_ref[t]
    nh2 = q_ref.shape[0]
    group = s_ref.shape[0]

    @pl.when(kj == 0)
    def _():
        m_ref[...] = jnp.full(m_ref.shape, M_INIT, F32)
        acc_ref[...] = jnp.zeros(acc_ref.shape, F32)

    def step(diagonal):
        kpos = (lax.broadcasted_iota(jnp.int32, (1, tq), 1) + (kj - qi) * tq).astype(F32)
        if diagonal:
            row = lax.broadcasted_iota(jnp.int32, (tq, tq), 0)
            col = lax.broadcasted_iota(jnp.int32, (tq, tq), 1)
            allowed = col <= row
        ones = jnp.ones((tq, V_DIM), BF)
        heads_per_pass = group // 2
        for h0 in range(0, N_HEADS, heads_per_pass):
            mhs = [m * N_HEADS + h for h in range(h0, h0 + heads_per_pass) for m in range(2)]
            for idx, mh in enumerate(mhs):
                s_ref[idx] = _mm(q_ref[mh], k_ref[mh])
            for idx, mh in enumerate(mhs):
                s = s_ref[idx] + slopes[mh % N_HEADS] * kpos
                if diagonal:
                    s = jnp.where(allowed, s, MASK_VALUE)
                m_old = m_ref[mh]
                m_new = jnp.maximum(m_old, jnp.max(s, axis=1, keepdims=True))
                alpha = jnp.exp2(m_old - m_new)
                p_ref[idx] = jnp.exp2(s - jnp.tile(m_new, (1, tq // LANES))).astype(BF)
                m_ref[mh] = m_new
                acc_ref[mh] = jnp.tile(alpha, (1, 2)) * acc_ref[mh]
            for idx, mh in enumerate(mhs):
                h = mh % N_HEADS
                v_ext = jnp.concatenate([v_ref[:, h * V_DIM:(h + 1) * V_DIM], ones], axis=1)
                acc_ref[mh] += _mm(p_ref[idx], v_ext)

    @pl.when(kj < qi)
    def _():
        step(False)

    @pl.when(kj == qi)
    def _():
        step(True)
        lam = _lambda(lq1, lk1, lq2, lk2, lam_init)
        for h in range(N_HEADS):
            o0 = acc_ref[h, :, :V_DIM] / acc_ref[h, :, V_DIM:]
            o1 = acc_ref[N_HEADS + h, :, :V_DIM] / acc_ref[N_HEADS + h, :, V_DIM:]
            o = _sub_norm(o0 - lam * o1, sg_ref[...], lam_init)
            o_ref[:, h * V_DIM:(h + 1) * V_DIM] = o.astype(BF)


def _prompt_attention(q_hm, k_t, v, lams, sg, lam_init, slopes):
    b, nh2, hd, t = k_t.shape
    dv = v.shape[2]
    tq = _block_rows(t, TQ_ATTN)
    nq = t // tq
    tri_q = np.asarray([qi for qi in range(nq) for _ in range(qi + 1)], np.int32)
    tri_k = np.asarray([kj for qi in range(nq) for kj in range(qi + 1)], np.int32)

    def const_spec(shape):
        return pl.BlockSpec(shape, lambda bi, ti, tqr, tkr: (0,) * len(shape))

    body = functools.partial(_flash_body, lam_init=lam_init, slopes=slopes, tq=tq)
    grid_spec = pltpu.PrefetchScalarGridSpec(
        num_scalar_prefetch=2,
        grid=(b, len(tri_q)),
        in_specs=[pl.BlockSpec((nh2, tq, hd), lambda bi, ti, tqr, tkr: (0, bi * nq + tqr[ti], 0)),
                  pl.BlockSpec((None, nh2, hd, tq), lambda bi, ti, tqr, tkr: (bi, 0, 0, tkr[ti])),
                  pl.BlockSpec((None, tq, dv), lambda bi, ti, tqr, tkr: (bi, tkr[ti], 0)),
                  *[const_spec((1, hd))] * 4, const_spec((1, V_DIM))],
        out_specs=pl.BlockSpec((None, tq, dv), lambda bi, ti, tqr, tkr: (bi, tqr[ti], 0)),
        scratch_shapes=[pltpu.VMEM((nh2, tq, LANES), F32), pltpu.VMEM((nh2, tq, 2 * V_DIM), F32),
                        pltpu.VMEM((HEAD_MAPS_PER_PASS, tq, tq), F32),
                        pltpu.VMEM((HEAD_MAPS_PER_PASS, tq, tq), BF)],
    )
    return pl.pallas_call(
        body,
        out_shape=jax.ShapeDtypeStruct((b, t, dv), BF),
        grid_spec=grid_spec,
        compiler_params=_params(("parallel", "arbitrary")),
        name="prompt_attention",
    )(jnp.asarray(tri_q), jnp.asarray(tri_k), q_hm, k_t, v, *lams, sg)


def _decode_body(pt_ref, q_ref, kn_ref, vn_ref, slope_ref, expand_ref, *rest, lam_init, past_len, pps):
    kt_refs = rest[:pps]
    vp_refs = rest[pps:2 * pps]
    lq1, lk1, lq2, lk2, sg_ref, o_ref, qb_ref, m_ref, l_ref, acc_ref = rest[2 * pps:]
    g = pl.program_id(1)
    nh2, hd = q_ref.shape

    @pl.when(g == 0)
    def _():
        eye = (lax.broadcasted_iota(jnp.int32, (hd, hd), 0) == lax.broadcasted_iota(jnp.int32, (hd, hd), 1))
        ones = jnp.ones((hd, LANES), BF)
        q = q_ref[...].astype(F32)
        for mh in range(nh2):
            qb_ref[mh] = _mm(jnp.where(eye, q[mh:mh + 1, :], 0.0).astype(BF), ones)
        s_self = jnp.sum(q * kn_ref[...], axis=1, keepdims=True)
        m_ref[...] = jnp.broadcast_to(s_self, m_ref.shape)
        l_ref[...] = jnp.ones(l_ref.shape, F32)
        acc_ref[...] = jnp.concatenate([vn_ref[...], vn_ref[...]], axis=0)

    row = lax.broadcasted_iota(jnp.int32, (nh2, PAGE * N_HEADS), 0)
    col = lax.broadcasted_iota(jnp.int32, (nh2, PAGE * N_HEADS), 1)
    own_head = row % N_HEADS == col % N_HEADS
    for jj in range(pps):
        first_key = (g * pps + jj) * PAGE - past_len
        kpos = (lax.broadcasted_iota(jnp.int32, (1, PAGE), 1) + first_key).astype(F32)
        s = jnp.sum(kt_refs[jj][...] * qb_ref[...], axis=1) + slope_ref[...] * kpos
        m_old = m_ref[...]
        m_new = jnp.maximum(m_old, jnp.max(s, axis=1, keepdims=True))
        alpha = jnp.exp2(m_old - m_new)
        p = jnp.exp2(s - m_new)
        l_ref[...] = alpha * l_ref[...] + jnp.sum(p, axis=1, keepdims=True)
        m_ref[...] = m_new
        p_rows = _mm(p.astype(BF), expand_ref[...])
        p_rows = jnp.where(own_head, p_rows, 0.0).astype(BF)
        v_rows = vp_refs[jj][...].reshape(PAGE * N_HEADS, V_DIM).astype(BF)
        acc_ref[...] = alpha * acc_ref[...] + _mm(p_rows, v_rows)

    @pl.when(g == pl.num_programs(1) - 1)
    def _():
        lam = _lambda(lq1, lk1, lq2, lk2, lam_init)
        o0 = acc_ref[:N_HEADS, :] / l_ref[:N_HEADS, :]
        o1 = acc_ref[N_HEADS:, :] / l_ref[N_HEADS:, :]
        o_ref[...] = _sub_norm(o0 - lam * o1, sg_ref[...], lam_init).astype(BF)


def _decode_attention(q_hm, k_new, v_new, cache_k, cache_v, page_table, lams, sg, lam_init, slopes):
    q = q_hm.transpose(1, 0, 2)
    r = q.shape[0]
    d = q.shape[1] * q.shape[2]
    n_pages = page_table.shape[1]
    pps = min(PAGES_PER_STEP, n_pages)
    assert n_pages % pps == 0 and cache_k.shape[1] == PAGE
    nh2 = 2 * N_HEADS
    slope_arr = jnp.asarray(np.tile(np.asarray(slopes, np.float32), 2)[:, None] * np.ones((1, LANES), np.float32))
    expand = jnp.asarray(np.repeat(np.eye(PAGE, dtype=np.float32), N_HEADS, axis=1), BF)
    cache_kt = jnp.transpose(cache_k, (0, 2, 3, 1))

    def k_spec(jj):
        return pl.BlockSpec((None, nh2, HEAD_DIM, PAGE), lambda ri, gi, pt: (pt[ri, gi * pps + jj], 0, 0, 0))

    def v_spec(jj):
        return pl.BlockSpec((None, PAGE, N_HEADS, V_DIM), lambda ri, gi, pt: (pt[ri, gi * pps + jj], 0, 0, 0))

    def tok_spec(rows, cols):
        return pl.BlockSpec((None, rows, cols), lambda ri, gi, pt: (ri, 0, 0))

    def const_spec(shape):
        return pl.BlockSpec(shape, lambda ri, gi, pt: (0,) * len(shape))

    body = functools.partial(_decode_body, lam_init=lam_init, past_len=n_pages * PAGE, pps=pps)
    grid_spec = pltpu.PrefetchScalarGridSpec(
        num_scalar_prefetch=1,
        grid=(r, n_pages // pps),
        in_specs=[tok_spec(nh2, HEAD_DIM), tok_spec(nh2, HEAD_DIM), tok_spec(N_HEADS, V_DIM),
                  const_spec((nh2, LANES)), const_spec(expand.shape),
                  *[k_spec(jj) for jj in range(pps)], *[v_spec(jj) for jj in range(pps)],
                  *[const_spec((1, HEAD_DIM))] * 4, const_spec((1, V_DIM))],
        out_specs=tok_spec(N_HEADS, V_DIM),
        scratch_shapes=[pltpu.VMEM((nh2, HEAD_DIM, LANES), F32),
                        pltpu.VMEM((nh2, LANES), F32), pltpu.VMEM((nh2, LANES), F32),
                        pltpu.VMEM((nh2, V_DIM), F32)],
    )
    out = pl.pallas_call(
        body,
        out_shape=jax.ShapeDtypeStruct((r, N_HEADS, V_DIM), BF),
        grid_spec=grid_spec,
        compiler_params=_params(("parallel", "arbitrary")),
        name="decode_attention",
    )(page_table, q, k_new.reshape(r, nh2, HEAD_DIM),
      v_new.reshape(r, N_HEADS, V_DIM), slope_arr, expand, *([cache_kt] * pps), *([cache_v] * pps), *lams, sg)
    return out.reshape(r, d)


def _top_rank(s, k, order=None):
    if order is None:
        order = lax.broadcasted_iota(jnp.int32, s.shape, 0).astype(F32)
    work = s
    rank = jnp.full(s.shape, float(k), F32)
    vals = []
    for a in range(k):
        m = jnp.max(work, axis=0, keepdims=True)
        first = jnp.min(jnp.where(work == m, order, jnp.inf), axis=0, keepdims=True)
        hit = order == first
        rank = jnp.where(hit, float(a), rank)
        work = jnp.where(hit, -jnp.inf, work)
        vals.append(m)
    return vals, rank


def _top_values(s, k):
    work = s
    vals = []
    for _ in range(k):
        m = jnp.max(work, axis=0, keepdims=True)
        work = jnp.where(work == m, -jnp.inf, work)
        vals.append(m)
    taken = jnp.sum(jnp.where(work != s, 1.0, 0.0), axis=0, keepdims=True)
    return vals, taken == float(k)


KQ = 4
SELECT_UNROLL = 4


def _candidates(v1, v2, k):
    assert KQ * KQ >= k
    v1_all = jnp.concatenate(v1, axis=0)
    v2_all = jnp.concatenate(v2, axis=0)
    ar = lax.broadcasted_iota(jnp.int32, v1_all.shape, 0)
    low_a = jnp.where(ar >= KQ, v1_all, -jnp.inf)
    return jnp.concatenate([v1[a] + v2_all for a in range(KQ)] + [low_a + v2[b] for b in range(KQ)], axis=0)


def _partner_counts(picked, k):
    count_hi = picked[KQ * k:(KQ + 1) * k]
    for b in range(1, KQ):
        count_hi = count_hi + picked[(KQ + b) * k:(KQ + b + 1) * k]
    return [jnp.sum(picked[a * k:(a + 1) * k], axis=0, keepdims=True) if a < KQ else count_hi[a:a + 1]
            for a in range(k)]


def _partner_lows(chosen, v2, k):
    v2_all = jnp.concatenate(v2, axis=0)
    low_hi = jnp.where(chosen[KQ * k:(KQ + 1) * k], v2[0], jnp.inf)
    for b in range(1, KQ):
        low_hi = jnp.minimum(low_hi, jnp.where(chosen[(KQ + b) * k:(KQ + b + 1) * k], v2[b], jnp.inf))
    return [jnp.min(jnp.where(chosen[a * k:(a + 1) * k], v2_all, jnp.inf), axis=0, keepdims=True) if a < KQ
            else low_hi[a:a + 1] for a in range(k)]


def _store_gate_tables(h, ts, a_ref, thr_ref, e2_ref, cmp_ref, s1, s2, max1, max2, top_sums, thr, cmp):
    z = jnp.ones_like(top_sums[0])
    for t in range(1, len(top_sums)):
        z = z + jnp.exp(top_sums[t] - top_sums[0])
    a_ref[h, ts] = jnp.exp(s1 - max1) / z
    thr_ref[h, ts] = thr
    e2_ref[h, ts] = jnp.exp(s2 - max2)
    cmp_ref[h, ts] = cmp


def _half_scores(h, ts, keys_ref, qt_ref):
    lanes = pl.ds(pl.multiple_of(ts * LANES, LANES), LANES)

    def scores(c):
        hc = 2 * h + c
        qq = qt_ref[pl.ds(pl.multiple_of(hc * N_KEYS, N_KEYS), N_KEYS), lanes]
        return _mm(keys_ref[hc], qq)

    return scores(0), scores(1)


def _peer_select(h, ts, keys_ref, qt_ref, a_ref, thr_ref, e2_ref, cmp_ref):
    k = PEER_TOPK
    s1, s2 = _half_scores(h, ts, keys_ref, qt_ref)
    v1, distinct1 = _top_values(s1, k)
    v2, distinct2 = _top_values(s2, k)
    cand = _candidates(v1, v2, k)
    vc, distinct_c = _top_values(cand, k)
    lows = _partner_lows(cand >= vc[k - 1], v2, k)
    thr = jnp.full(s1.shape, jnp.inf, F32)
    for a in range(k):
        thr = jnp.where(s1 == v1[a], lows[a], thr)
    _store_gate_tables(h, ts, a_ref, thr_ref, e2_ref, cmp_ref, s1, s2, v1[0], v2[0], vc, thr, s2)
    return jnp.min(jnp.where(distinct1 & distinct2 & distinct_c, 1.0, 0.0)) > 0.5


def _peer_select_ranked(h, ts, keys_ref, qt_ref, a_ref, thr_ref, e2_ref, cmp_ref):
    k = PEER_TOPK
    s1, s2 = _half_scores(h, ts, keys_ref, qt_ref)
    v1, r1 = _top_rank(s1, k)
    v2, r2 = _top_rank(s2, k)
    cand = _candidates(v1, v2, k)
    row = lax.broadcasted_iota(jnp.int32, cand.shape, 0)
    rb = row - KQ * k
    order = jnp.where(row < KQ * k, row, (rb % k) * k + rb // k).astype(F32)
    vc, rc = _top_rank(cand, k, order)
    counts = _partner_counts(jnp.where(rc < float(k), 1.0, 0.0), k)
    n = jnp.zeros_like(s1)
    for a in range(k):
        n = jnp.where(r1 == float(a), counts[a], n)
    _store_gate_tables(h, ts, a_ref, thr_ref, e2_ref, cmp_ref, s1, s2, v1[0], v2[0], vc, 0.5 - n, -r2)


def _peer_body(x_ref, g_ref, wqt_ref, keys_ref, u_ref, vt_ref, o_ref,
               ht_ref, qt_ref, acc_ref, act0_ref, act1_ref, gate0_ref, gate1_ref, a_ref, thr_ref, e2_ref, cmp_ref):
    s = pl.program_id(1)
    n_blocks = pl.num_programs(1) - 2
    tm = x_ref.shape[0]
    te = u_ref.shape[0]
    nts = tm // LANES

    @pl.when(s == 0)
    def _():
        hn = _rms(x_ref[...], g_ref[...])
        ht = hn.T.astype(BF)
        ht_ref[...] = ht
        qt_ref[...] = _mm(wqt_ref[...], ht).astype(BF)
        acc_ref[...] = jnp.zeros(acc_ref.shape, F32)
        gate1_ref[...] = jnp.zeros(gate1_ref.shape, BF)

        tables = (keys_ref, qt_ref, a_ref, thr_ref, e2_ref, cmp_ref)

        def select(it, carry):
            tiles = [((it * SELECT_UNROLL + u) // nts, (it * SELECT_UNROLL + u) % nts) for u in range(SELECT_UNROLL)]
            no_ties = [_peer_select(h, ts, *tables) for h, ts in tiles]
            for (h, ts), ok in zip(tiles, no_ties):
                @pl.when(jnp.logical_not(ok))
                def _():
                    _peer_select_ranked(h, ts, *tables)
            return carry

        assert (PEER_HEADS * nts) % SELECT_UNROLL == 0
        lax.fori_loop(0, PEER_HEADS * nts // SELECT_UNROLL, select, 0)

    gate_block = jnp.clip(s - 1, 0, n_blocks - 1)

    def stages(act_cur, act_prev, gate_cur, gate_prev, do_act=True, do_gate=True, do_value=True):
        for ii in range(te // N_KEYS):
            lo, hi = ii * N_KEYS, (ii + 1) * N_KEYS
            if do_act and lo % ACT_ROWS == 0:
                rows = slice(lo, lo + ACT_ROWS)
                act_cur[rows, :] = _mm(u_ref[rows, :], ht_ref[...])
            if do_gate:
                i = gate_block * (te // N_KEYS) + ii
                for lt in range(nts):
                    ls = slice(lt * LANES, (lt + 1) * LANES)
                    w = None
                    for h in range(PEER_HEADS):
                        a_row = a_ref[h, lt, pl.ds(i, 1), :]
                        thr_row = thr_ref[h, lt, pl.ds(i, 1), :]
                        t = jnp.where(cmp_ref[h, lt] >= thr_row, e2_ref[h, lt] * a_row, 0.0)
                        w = t if w is None else w + t
                    gate_cur[lo:hi, ls] = (_gelu(act_prev[lo:hi, ls]) * w).astype(BF)
            if do_value and ii in VALUE_AFTER:
                c = VALUE_AFTER.index(ii)
                ks = slice(c * VALUE_DEPTH, (c + 1) * VALUE_DEPTH)
                acc_ref[...] += _mm(vt_ref[:, ks], gate_prev[ks, :])

    full = jnp.logical_and(s > 0, s <= n_blocks)

    @pl.when(s == 0)
    def _():
        stages(act0_ref, None, None, None, do_gate=False, do_value=False)

    @pl.when(jnp.logical_and(full, s % 2 == 0))
    def _():
        stages(act0_ref, act1_ref, gate1_ref, gate0_ref)

    @pl.when(jnp.logical_and(full, s % 2 == 1))
    def _():
        stages(act1_ref, act0_ref, gate0_ref, gate1_ref)

    @pl.when(s == n_blocks + 1)
    def _():
        stages(None, None, None, gate1_ref, do_act=False, do_gate=False)

    @pl.when(s == pl.num_programs(1) - 1)
    def _():
        o_ref[...] = x_ref[...] + acc_ref[...].T


def _peer(x, g, wqt, keys, u, vt):
    n, d = x.shape
    n_exp = u.shape[0]
    tm = _block_rows(n, TM_PEER)
    te = vt.shape[2]
    assert tm % LANES == 0 and n_exp % te == 0 and te % (2 * N_KEYS) == 0
    table = pltpu.VMEM((PEER_HEADS, tm // LANES, N_KEYS, LANES), F32)
    act = pltpu.VMEM((te, tm), F32)
    gate = pltpu.VMEM((te, tm), BF)
    nb = n_exp // te
    assert nb % 2 == 0 and te % ACT_ROWS == 0 and te % VALUE_DEPTH == 0
    return pl.pallas_call(
        _peer_body,
        out_shape=jax.ShapeDtypeStruct((n, d), F32),
        grid=(n // tm, nb + 2),
        in_specs=[pl.BlockSpec((tm, d), lambda i, j: (i, 0)),
                  pl.BlockSpec((1, d), lambda i, j: (0, 0)),
                  pl.BlockSpec(wqt.shape, lambda i, j: (0, 0)),
                  pl.BlockSpec(keys.shape, lambda i, j: (0, 0, 0)),
                  pl.BlockSpec((te, d), lambda i, j: (jnp.minimum(j, nb - 1), 0)),
                  pl.BlockSpec((None, d, te), lambda i, j: (jnp.clip(j - 2, 0, nb - 1), 0, 0))],
        out_specs=pl.BlockSpec((tm, d), lambda i, j: (i, 0)),
        scratch_shapes=[pltpu.VMEM((d, tm), BF), pltpu.VMEM((wqt.shape[0], tm), BF),
                        pltpu.VMEM((d, tm), F32), act, act, gate, gate,
                        table, table, table, table],
        compiler_params=_params(("parallel", "arbitrary")),
        name="peer",
    )(x, g.reshape(1, d), wqt, keys, u, vt)


def kernel(x_prompt, x_sample, cache_k, cache_v, page_table, a_norm_g, a_w_in, a_vnorm_g, a_w_s, a_b_s, a_w_out,
           kv_norm_g, w_k, w_v, k_norm_g, b_norm_g, w_q, q_norm_g, lambda_q1, lambda_k1, lambda_q2, lambda_k2,
           subln_g, w_o, f_norm_g, peer_w_q, peer_keys, peer_u, peer_v):
    bsz, seq, d = x_prompt.shape
    r, dec_seq, _ = x_sample.shape
    assert dec_seq == 1 and seq % CHUNK == 0
    depth = f_norm_g.shape[0]
    n_a = a_norm_g.shape[0]
    nh2 = 2 * N_HEADS
    slopes = tuple(float(v) * LOG2E for v in 2.0 ** (-8.0 * np.arange(1, N_HEADS + 1, dtype=np.float32) / N_HEADS))

    xp = x_prompt.reshape(bsz * seq, d)
    xs = x_sample.reshape(r, d)
    seg = _seg_mean_matrix(nh2 * HEAD_DIM, HEAD_DIM)
    gmlp_v_rows = []
    for l in range(depth):
        if l < n_a:
            win = a_w_in[l].astype(BF)
            wout = a_w_out[l].astype(BF)
            xp = _gmlp_prompt(xp, a_norm_g[l], win, a_vnorm_g[l], a_w_s[l], a_b_s[l], wout)
            xs, vs = _gmlp_sample(xs, a_norm_g[l], win, a_vnorm_g[l], a_w_s[l], a_b_s[l], wout)
            gmlp_v_rows.append(vs.reshape(r, dec_seq, -1))
        else:
            if l == n_a:
                wk = w_k.astype(BF)
                wv = w_v.astype(BF)
                kg = jnp.tile(k_norm_g, nh2).reshape(1, nh2 * HEAD_DIM)
                kp_t, vp, kp_tb, vp_b = _shared_kv(xp, kv_norm_g, wk, wv, seg, kg, batch=bsz)
                ks, vs_new, _, _ = _shared_kv(xs, kv_norm_g, wk, wv, seg, kg)
                k_t = kp_tb.reshape(bsz, nh2, HEAD_DIM, seq)
                v_b = vp_b.reshape(bsz, seq, N_HEADS * V_DIM)
            i = l - n_a
            lam_init = 0.8 - 0.6 * math.exp(-0.3 * l)
            wq = w_q[i].astype(BF)
            qg = jnp.tile(q_norm_g[i], nh2).reshape(1, nh2 * HEAD_DIM)
            lams = [v[i].reshape(1, HEAD_DIM) for v in (lambda_q1, lambda_k1, lambda_q2, lambda_k2)]
            sg = subln_g[i].reshape(1, V_DIM)
            wo = w_o[i].astype(BF)
            q_hm = _q_proj(xp, b_norm_g[i], wq, seg, qg)
            op = _prompt_attention(q_hm, k_t, v_b, lams, sg, lam_init, slopes)
            xp = _o_proj(xp, op.reshape(bsz * seq, -1), wo)
            qs = _q_proj(xs, b_norm_g[i], wq, seg, qg)
            os_ = _decode_attention(qs, ks, vs_new, cache_k, cache_v, page_table, lams, sg, lam_init, slopes)
            xs = _o_proj(xs, os_, wo)
        wqt = peer_w_q[l].T.astype(BF)
        keys = peer_keys[l].reshape(PEER_HEADS * 2, N_KEYS, -1).astype(BF)
        u = peer_u[l].astype(BF)
        vt = peer_v[l].reshape(-1, TE_PEER, d).transpose(0, 2, 1).astype(BF)
        xp = _peer(xp, f_norm_g[l], wqt, keys, u, vt)
        xs = _peer(xs, f_norm_g[l], wqt, keys, u, vt)

    state_gmlp_v = jnp.stack(gmlp_v_rows, axis=0)
    return (xp.reshape(bsz, seq, d), xs.reshape(r, dec_seq, d),
            kp_t.reshape(bsz, nh2, HEAD_DIM, seq).transpose(0, 3, 1, 2), vp.reshape(bsz, seq, N_HEADS, V_DIM),
            ks.reshape(r, dec_seq, nh2, HEAD_DIM), vs_new.reshape(r, dec_seq, N_HEADS, V_DIM),
            state_gmlp_v)
```

```python
import functools
import math

import numpy as np
import jax
import jax.numpy as jnp
from jax import lax
from jax.experimental import pallas as pl
from jax.experimental.pallas import tpu as pltpu

F32 = jnp.float32
BF = jnp.bfloat16

EPS = 1e-6
MASK_VALUE = -1e9
M_INIT = -1e30
LOG2E = math.log2(math.e)

LANES = 128
CHUNK = 128
N_GROUPS = 8
N_HEADS = 8
HEAD_DIM = 64
V_DIM = 128
PAGE = 128
PEER_HEADS = 8
N_KEYS = 128
PEER_TOPK = 16

VMEM_LIMIT = 56 * 1024 * 1024

TM_DENSE = 512
TM_PEER = 512
TE_PEER = 512
ACT_ROWS = 512
VALUE_DEPTH = 256
VALUE_AFTER = (1, 3)
TQ_ATTN = 512
HEAD_MAPS_PER_PASS = 16
PAGES_PER_STEP = 16


def _params(sem):
    return pltpu.CompilerParams(dimension_semantics=sem, vmem_limit_bytes=VMEM_LIMIT)


def _gelu(x):
    half = 0.5 * x
    return half + half * jnp.tanh(x * (0.7978845608028654 + (0.7978845608028654 * 0.044715) * (x * x)))


def _rms(x, g):
    ms = jnp.mean(x * x, axis=-1, keepdims=True)
    return x * lax.rsqrt(ms + EPS) * g


def _mm(a, b):
    return jnp.dot(a, b, preferred_element_type=F32)


def _row_spec(tm, d):
    return pl.BlockSpec((tm, d), lambda i: (i, 0))


def _full_spec(shape):
    nd = len(shape)
    return pl.BlockSpec(shape, lambda *_: (0,) * nd)


def _block_rows(n, pref):
    tm = min(pref, n)
    assert n % tm == 0, (n, tm)
    return tm


def _gmlp_front(x_ref, ng_ref, win_ref, vg_ref):
    x = x_ref[...]
    xn = _rms(x, ng_ref[...]).astype(BF)
    z = _gelu(_mm(xn, win_ref[...]))
    gd = z.shape[1] // 2
    return x, z[:, :gd], _rms(z[:, gd:], vg_ref[...])


def _gmlp_prompt_body(x_ref, ng_ref, win_ref, vg_ref, ws_ref, bias_ref, wout_ref, xo_ref, gated_ref):
    x, u, vn = _gmlp_front(x_ref, ng_ref, win_ref, vg_ref)
    vb = vn.astype(BF)
    tm, gd = u.shape
    gw = gd // N_GROUPS
    row = lax.broadcasted_iota(jnp.int32, (CHUNK, CHUNK), 0)
    col = lax.broadcasted_iota(jnp.int32, (CHUNK, CHUNK), 1)
    causal = col <= row
    for g in range(N_GROUPS):
        w = jnp.where(causal, ws_ref[g], 0.0).astype(BF)
        for c in range(tm // CHUNK):
            rs = slice(c * CHUNK, (c + 1) * CHUNK)
            cs = slice(g * gw, (g + 1) * gw)
            mixed = _mm(w, vb[rs, cs]) + bias_ref[:, cs]
            gated_ref[rs, cs] = (u[rs, cs] * mixed).astype(BF)
    xo_ref[...] = x + _mm(gated_ref[...], wout_ref[...])


def _gmlp_sample_body(x_ref, ng_ref, win_ref, vg_ref, w0_ref, b0_ref, wout_ref, xo_ref, vn_ref):
    x, u, vn = _gmlp_front(x_ref, ng_ref, win_ref, vg_ref)
    vn_ref[...] = vn
    gated = (u * (vn * w0_ref[...] + b0_ref[...])).astype(BF)
    xo_ref[...] = x + _mm(gated, wout_ref[...])


def _gmlp_prompt(x, ng, win, vg, ws, bs, wout):
    n, d = x.shape
    gd = wout.shape[0]
    tm = _block_rows(n, TM_DENSE)
    assert tm % CHUNK == 0
    bias = jnp.repeat(bs.T, gd // N_GROUPS, axis=1)
    return pl.pallas_call(
        _gmlp_prompt_body,
        out_shape=jax.ShapeDtypeStruct((n, d), F32),
        grid=(n // tm,),
        in_specs=[_row_spec(tm, d), _full_spec((1, d)), _full_spec(win.shape), _full_spec((1, gd)),
                  _full_spec(ws.shape), _full_spec(bias.shape), _full_spec(wout.shape)],
        out_specs=_row_spec(tm, d),
        scratch_shapes=[pltpu.VMEM((tm, gd), BF)],
        compiler_params=_params(("parallel",)),
        name="gmlp_prompt",
    )(x, ng.reshape(1, d), win, vg.reshape(1, gd), ws, bias, wout)


def _gmlp_sample(x, ng, win, vg, ws, bs, wout):
    n, d = x.shape
    gd = wout.shape[0]
    gw = gd // N_GROUPS
    tm = _block_rows(n, TM_DENSE)
    w0 = jnp.repeat(ws[:, 0, 0], gw).reshape(1, gd)
    b0 = jnp.repeat(bs[:, 0], gw).reshape(1, gd)
    return pl.pallas_call(
        _gmlp_sample_body,
        out_shape=(jax.ShapeDtypeStruct((n, d), F32), jax.ShapeDtypeStruct((n, gd), F32)),
        grid=(n // tm,),
        in_specs=[_row_spec(tm, d), _full_spec((1, d)), _full_spec(win.shape), _full_spec((1, gd)),
                  _full_spec((1, gd)), _full_spec((1, gd)), _full_spec(wout.shape)],
        out_specs=(_row_spec(tm, d), _row_spec(tm, gd)),
        compiler_params=_params(("parallel",)),
        name="gmlp_sample",
    )(x, ng.reshape(1, d), win, vg.reshape(1, gd), w0, b0, wout)


def _head_norm(y, seg_ref, gt):
    y2 = y * y
    hi = y2.astype(BF)
    lo = (y2 - hi.astype(F32)).astype(BF)
    ms = _mm(hi, seg_ref[...]) + _mm(lo, seg_ref[...])
    return y * lax.rsqrt(ms + EPS) * gt


def _seg_mean_matrix(d, seg):
    idx = np.arange(d) // seg
    return jnp.asarray((idx[:, None] == idx[None, :]).astype(np.float32) / seg, BF)


def _kv_body(x_ref, g_ref, wk_ref, wv_ref, seg_ref, kg_ref, k_ref, v_ref, kb_ref, vb_ref, *, keys_transposed):
    h = _rms(x_ref[...], g_ref[...]).astype(BF)
    k = _head_norm(_mm(h, wk_ref[...]), seg_ref, kg_ref[...])
    v = _mm(h, wv_ref[...])
    if keys_transposed:
        k = k.T
    k_ref[...] = k
    v_ref[...] = v
    kb_ref[...] = k.astype(BF)
    vb_ref[...] = v.astype(BF)


def _shared_kv(x, g, wk, wv, seg, kg, batch=None):
    n, d = x.shape
    dk, dv = wk.shape[1], wv.shape[1]
    tm = _block_rows(n, TM_DENSE)
    if batch is None:
        k_shape, k_spec = (n, dk), _row_spec(tm, dk)
    else:
        seq = n // batch
        assert seq % tm == 0
        k_shape = (batch, dk, seq)
        k_spec = pl.BlockSpec((None, dk, tm), lambda i: (i // (seq // tm), 0, i % (seq // tm)))
    return pl.pallas_call(
        functools.partial(_kv_body, keys_transposed=batch is not None),
        out_shape=(jax.ShapeDtypeStruct(k_shape, F32), jax.ShapeDtypeStruct((n, dv), F32),
                   jax.ShapeDtypeStruct(k_shape, BF), jax.ShapeDtypeStruct((n, dv), BF)),
        grid=(n // tm,),
        in_specs=[_row_spec(tm, d), _full_spec((1, d)), _full_spec(wk.shape), _full_spec(wv.shape),
                  _full_spec(seg.shape), _full_spec((1, dk))],
        out_specs=(k_spec, _row_spec(tm, dv), k_spec, _row_spec(tm, dv)),
        compiler_params=_params(("parallel",)),
        name="shared_kv",
    )(x, g.reshape(1, d), wk, wv, seg, kg)


def _q_body(x_ref, g_ref, wq_ref, seg_ref, qg_ref, q_ref):
    h = _rms(x_ref[...], g_ref[...]).astype(BF)
    q = _head_norm(_mm(h, wq_ref[...]), seg_ref, qg_ref[...])
    q = (q * (HEAD_DIM ** -0.5 * LOG2E)).astype(BF)
    for mh in range(q_ref.shape[0]):
        q_ref[mh] = q[:, mh * HEAD_DIM:(mh + 1) * HEAD_DIM]


def _q_proj(x, g, wq, seg, qg):
    n, d = x.shape
    dq = wq.shape[1]
    tm = _block_rows(n, TM_DENSE)
    return pl.pallas_call(
        _q_body,
        out_shape=jax.ShapeDtypeStruct((dq // HEAD_DIM, n, HEAD_DIM), BF),
        grid=(n // tm,),
        in_specs=[_row_spec(tm, d), _full_spec((1, d)), _full_spec(wq.shape), _full_spec(seg.shape),
                  _full_spec((1, dq))],
        out_specs=pl.BlockSpec((dq // HEAD_DIM, tm, HEAD_DIM), lambda i: (0, i, 0)),
        compiler_params=_params(("parallel",)),
        name="q_proj",
    )(x, g.reshape(1, d), wq, seg, qg)


def _oproj_body(x_ref, o_ref, w_ref, xo_ref):
    xo_ref[...] = x_ref[...] + _mm(o_ref[...], w_ref[...])


def _o_proj(x, o, w):
    n, d = x.shape
    tm = _block_rows(n, TM_DENSE)
    return pl.pallas_call(
        _oproj_body,
        out_shape=jax.ShapeDtypeStruct((n, d), F32),
        grid=(n // tm,),
        in_specs=[_row_spec(tm, d), _row_spec(tm, o.shape[1]), _full_spec(w.shape)],
        out_specs=_row_spec(tm, d),
        compiler_params=_params(("parallel",)),
        name="o_proj",
    )(x, o, w)


def _lambda(lq1, lk1, lq2, lk2, lam_init):
    a = jnp.sum(lq1[...] * lk1[...], axis=-1, keepdims=True)
    b = jnp.sum(lq2[...] * lk2[...], axis=-1, keepdims=True)
    return jnp.exp(a) - jnp.exp(b) + lam_init


def _sub_norm(o, sg, lam_init):
    ms = jnp.mean(o * o, axis=-1, keepdims=True)
    return o * lax.rsqrt(ms + EPS) * sg * (1.0 - lam_init)


def _flash_body(tq_ref, tk_ref, q_ref, k_ref, v_ref, lq1, lk1, lq2, lk2, sg_ref, o_ref,
                m_ref, acc_ref, s_ref, p_ref, *, lam_init, slopes, tq):
    t = pl.program_id(1)
    qi = tq_ref[t]
    kj = tk_ref[t]
    nh2 = q_ref.shape[0]
    group = s_ref.shape[0]

    @pl.when(kj == 0)
    def _():
        m_ref[...] = jnp.full(m_ref.shape, M_INIT, F32)
        acc_ref[...] = jnp.zeros(acc_ref.shape, F32)

    def step(diagonal):
        kpos = (lax.broadcasted_iota(jnp.int32, (1, tq), 1) + (kj - qi) * tq).astype(F32)
        if diagonal:
            row = lax.broadcasted_iota(jnp.int32, (tq, tq), 0)
            col = lax.broadcasted_iota(jnp.int32, (tq, tq), 1)
            allowed = col <= row
        ones = jnp.ones((tq, V_DIM), BF)
        heads_per_pass = group // 2
        for h0 in range(0, N_HEADS, heads_per_pass):
            mhs = [m * N_HEADS + h for h in range(h0, h0 + heads_per_pass) for m in range(2)]
            for idx, mh in enumerate(mhs):
                s_ref[idx] = _mm(q_ref[mh], k_ref[mh])
            for idx, mh in enumerate(mhs):
                s = s_ref[idx] + slopes[mh % N_HEADS] * kpos
                if diagonal:
                    s = jnp.where(allowed, s, MASK_VALUE)
                m_old = m_ref[mh]
                m_new = jnp.maximum(m_old, jnp.max(s, axis=1, keepdims=True))
                alpha = jnp.exp2(m_old - m_new)
                p_ref[idx] = jnp.exp2(s - jnp.tile(m_new, (1, tq // LANES))).astype(BF)
                m_ref[mh] = m_new
                acc_ref[mh] = jnp.tile(alpha, (1, 2)) * acc_ref[mh]
            for idx, mh in enumerate(mhs):
                h = mh % N_HEADS
                v_ext = jnp.concatenate([v_ref[:, h * V_DIM:(h + 1) * V_DIM], ones], axis=1)
                acc_ref[mh] += _mm(p_ref[idx], v_ext)

    @pl.when(kj < qi)
    def _():
        step(False)

    @pl.when(kj == qi)
    def _():
        step(True)
        lam = _lambda(lq1, lk1, lq2, lk2, lam_init)
        for h in range(N_HEADS):
            o0 = acc_ref[h, :, :V_DIM] / acc_ref[h, :, V_DIM:]
            o1 = acc_ref[N_HEADS + h, :, :V_DIM] / acc_ref[N_HEADS + h, :, V_DIM:]
            o = _sub_norm(o0 - lam * o1, sg_ref[...], lam_init)
            o_ref[:, h * V_DIM:(h + 1) * V_DIM] = o.astype(BF)


def _prompt_attention(q_hm, k_t, v, lams, sg, lam_init, slopes):
    b, nh2, hd, t = k_t.shape
    dv = v.shape[2]
    tq = _block_rows(t, TQ_ATTN)
    nq = t // tq
    tri_q = np.asarray([qi for qi in range(nq) for _ in range(qi + 1)], np.int32)
    tri_k = np.asarray([kj for qi in range(nq) for kj in range(qi + 1)], np.int32)

    def const_spec(shape):
        return pl.BlockSpec(shape, lambda bi, ti, tqr, tkr: (0,) * len(shape))

    body = functools.partial(_flash_body, lam_init=lam_init, slopes=slopes, tq=tq)
    grid_spec = pltpu.PrefetchScalarGridSpec(
        num_scalar_prefetch=2,
        grid=(b, len(tri_q)),
        in_specs=[pl.BlockSpec((nh2, tq, hd), lambda bi, ti, tqr, tkr: (0, bi * nq + tqr[ti], 0)),
                  pl.BlockSpec((None, nh2, hd, tq), lambda bi, ti, tqr, tkr: (bi, 0, 0, tkr[ti])),
                  pl.BlockSpec((None, tq, dv), lambda bi, ti, tqr, tkr: (bi, tkr[ti], 0)),
                  *[const_spec((1, hd))] * 4, const_spec((1, V_DIM))],
        out_specs=pl.BlockSpec((None, tq, dv), lambda bi, ti, tqr, tkr: (bi, tqr[ti], 0)),
        scratch_shapes=[pltpu.VMEM((nh2, tq, LANES), F32), pltpu.VMEM((nh2, tq, 2 * V_DIM), F32),
                        pltpu.VMEM((HEAD_MAPS_PER_PASS, tq, tq), F32),
                        pltpu.VMEM((HEAD_MAPS_PER_PASS, tq, tq), BF)],
    )
    return pl.pallas_call(
        body,
        out_shape=jax.ShapeDtypeStruct((b, t, dv), BF),
        grid_spec=grid_spec,
        compiler_params=_params(("parallel", "arbitrary")),
        name="prompt_attention",
    )(jnp.asarray(tri_q), jnp.asarray(tri_k), q_hm, k_t, v, *lams, sg)


def _decode_body(pt_ref, q_ref, kn_ref, vn_ref, slope_ref, expand_ref, *rest, lam_init, past_len, pps):
    kt_refs = rest[:pps]
    vp_refs = rest[pps:2 * pps]
    lq1, lk1, lq2, lk2, sg_ref, o_ref, qb_ref, m_ref, l_ref, acc_ref = rest[2 * pps:]
    g = pl.program_id(1)
    nh2, hd = q_ref.shape

    @pl.when(g == 0)
    def _():
        eye = (lax.broadcasted_iota(jnp.int32, (hd, hd), 0) == lax.broadcasted_iota(jnp.int32, (hd, hd), 1))
        ones = jnp.ones((hd, LANES), BF)
        q = q_ref[...].astype(F32)
        for mh in range(nh2):
            qb_ref[mh] = _mm(jnp.where(eye, q[mh:mh + 1, :], 0.0).astype(BF), ones)
        s_self = jnp.sum(q * kn_ref[...], axis=1, keepdims=True)
        m_ref[...] = jnp.broadcast_to(s_self, m_ref.shape)
        l_ref[...] = jnp.ones(l_ref.shape, F32)
        acc_ref[...] = jnp.concatenate([vn_ref[...], vn_ref[...]], axis=0)

    row = lax.broadcasted_iota(jnp.int32, (nh2, PAGE * N_HEADS), 0)
    col = lax.broadcasted_iota(jnp.int32, (nh2, PAGE * N_HEADS), 1)
    own_head = row % N_HEADS == col % N_HEADS
    for jj in range(pps):
        first_key = (g * pps + jj) * PAGE - past_len
        kpos = (lax.broadcasted_iota(jnp.int32, (1, PAGE), 1) + first_key).astype(F32)
        s = jnp.sum(kt_refs[jj][...] * qb_ref[...], axis=1) + slope_ref[...] * kpos
        m_old = m_ref[...]
        m_new = jnp.maximum(m_old, jnp.max(s, axis=1, keepdims=True))
        alpha = jnp.exp2(m_old - m_new)
        p = jnp.exp2(s - m_new)
        l_ref[...] = alpha * l_ref[...] + jnp.sum(p, axis=1, keepdims=True)
        m_ref[...] = m_new
        p_rows = _mm(p.astype(BF), expand_ref[...])
        p_rows = jnp.where(own_head, p_rows, 0.0).astype(BF)
        v_rows = vp_refs[jj][...].reshape(PAGE * N_HEADS, V_DIM).astype(BF)
        acc_ref[...] = alpha * acc_ref[...] + _mm(p_rows, v_rows)

    @pl.when(g == pl.num_programs(1) - 1)
    def _():
        lam = _lambda(lq1, lk1, lq2, lk2, lam_init)
        o0 = acc_ref[:N_HEADS, :] / l_ref[:N_HEADS, :]
        o1 = acc_ref[N_HEADS:, :] / l_ref[N_HEADS:, :]
        o_ref[...] = _sub_norm(o0 - lam * o1, sg_ref[...], lam_init).astype(BF)


def _decode_attention(q_hm, k_new, v_new, cache_k, cache_v, page_table, lams, sg, lam_init, slopes):
    q = q_hm.transpose(1, 0, 2)
    r = q.shape[0]
    d = q.shape[1] * q.shape[2]
    n_pages = page_table.shape[1]
    pps = min(PAGES_PER_STEP, n_pages)
    assert n_pages % pps == 0 and cache_k.shape[1] == PAGE
    nh2 = 2 * N_HEADS
    slope_arr = jnp.asarray(np.tile(np.asarray(slopes, np.float32), 2)[:, None] * np.ones((1, LANES), np.float32))
    expand = jnp.asarray(np.repeat(np.eye(PAGE, dtype=np.float32), N_HEADS, axis=1), BF)
    cache_kt = jnp.transpose(cache_k, (0, 2, 3, 1))

    def k_spec(jj):
        return pl.BlockSpec((None, nh2, HEAD_DIM, PAGE), lambda ri, gi, pt: (pt[ri, gi * pps + jj], 0, 0, 0))

    def v_spec(jj):
        return pl.BlockSpec((None, PAGE, N_HEADS, V_DIM), lambda ri, gi, pt: (pt[ri, gi * pps + jj], 0, 0, 0))

    def tok_spec(rows, cols):
        return pl.BlockSpec((None, rows, cols), lambda ri, gi, pt: (ri, 0, 0))

    def const_spec(shape):
        return pl.BlockSpec(shape, lambda ri, gi, pt: (0,) * len(shape))

    body = functools.partial(_decode_body, lam_init=lam_init, past_len=n_pages * PAGE, pps=pps)
    grid_spec = pltpu.PrefetchScalarGridSpec(
        num_scalar_prefetch=1,
        grid=(r, n_pages // pps),
        in_specs=[tok_spec(nh2, HEAD_DIM), tok_spec(nh2, HEAD_DIM), tok_spec(N_HEADS, V_DIM),
                  const_spec((nh2, LANES)), const_spec(expand.shape),
                  *[k_spec(jj) for jj in range(pps)], *[v_spec(jj) for jj in range(pps)],
                  *[const_spec((1, HEAD_DIM))] * 4, const_spec((1, V_DIM))],
        out_specs=tok_spec(N_HEADS, V_DIM),
        scratch_shapes=[pltpu.VMEM((nh2, HEAD_DIM, LANES), F32),
                        pltpu.VMEM((nh2, LANES), F32), pltpu.VMEM((nh2, LANES), F32),
                        pltpu.VMEM((nh2, V_DIM), F32)],
    )
    out = pl.pallas_call(
        body,
        out_shape=jax.ShapeDtypeStruct((r, N_HEADS, V_DIM), BF),
        grid_spec=grid_spec,
        compiler_params=_params(("parallel", "arbitrary")),
        name="decode_attention",
    )(page_table, q, k_new.reshape(r, nh2, HEAD_DIM),
      v_new.reshape(r, N_HEADS, V_DIM), slope_arr, expand, *([cache_kt] * pps), *([cache_v] * pps), *lams, sg)
    return out.reshape(r, d)


def _top_rank(s, k, order=None):
    if order is None:
        order = lax.broadcasted_iota(jnp.int32, s.shape, 0).astype(F32)
    work = s
    rank = jnp.full(s.shape, float(k), F32)
    vals = []
    for a in range(k):
        m = jnp.max(work, axis=0, keepdims=True)
        first = jnp.min(jnp.where(work == m, order, jnp.inf), axis=0, keepdims=True)
        hit = order == first
        rank = jnp.where(hit, float(a), rank)
        work = jnp.where(hit, -jnp.inf, work)
        vals.append(m)
    return vals, rank


def _top_values(s, k):
    work = s
    vals = []
    for _ in range(k):
        m = jnp.max(work, axis=0, keepdims=True)
        work = jnp.where(work == m, -jnp.inf, work)
        vals.append(m)
    taken = jnp.sum(jnp.where(work != s, 1.0, 0.0), axis=0, keepdims=True)
    return vals, taken == float(k)


KQ = 4
SELECT_UNROLL = 4


def _candidates(v1, v2, k):
    assert KQ * KQ >= k
    v1_all = jnp.concatenate(v1, axis=0)
    v2_all = jnp.concatenate(v2, axis=0)
    ar = lax.broadcasted_iota(jnp.int32, v1_all.shape, 0)
    low_a = jnp.where(ar >= KQ, v1_all, -jnp.inf)
    return jnp.concatenate([v1[a] + v2_all for a in range(KQ)] + [low_a + v2[b] for b in range(KQ)], axis=0)


def _partner_counts(picked, k):
    count_hi = picked[KQ * k:(KQ + 1) * k]
    for b in range(1, KQ):
        count_hi = count_hi + picked[(KQ + b) * k:(KQ + b + 1) * k]
    return [jnp.sum(picked[a * k:(a + 1) * k], axis=0, keepdims=True) if a < KQ else count_hi[a:a + 1]
            for a in range(k)]


def _partner_lows(chosen, v2, k):
    v2_all = jnp.concatenate(v2, axis=0)
    low_hi = jnp.where(chosen[KQ * k:(KQ + 1) * k], v2[0], jnp.inf)
    for b in range(1, KQ):
        low_hi = jnp.minimum(low_hi, jnp.where(chosen[(KQ + b) * k:(KQ + b + 1) * k], v2[b], jnp.inf))
    return [jnp.min(jnp.where(chosen[a * k:(a + 1) * k], v2_all, jnp.inf), axis=0, keepdims=True) if a < KQ
            else low_hi[a:a + 1] for a in range(k)]


def _store_gate_tables(h, ts, a_ref, thr_ref, e2_ref, cmp_ref, s1, s2, max1, max2, top_sums, thr, cmp):
    z = jnp.ones_like(top_sums[0])
    for t in range(1, len(top_sums)):
        z = z + jnp.exp(top_sums[t] - top_sums[0])
    a_ref[h, ts] = jnp.exp(s1 - max1) / z
    thr_ref[h, ts] = thr
    e2_ref[h, ts] = jnp.exp(s2 - max2)
    cmp_ref[h, ts] = cmp


def _half_scores(h, ts, keys_ref, qt_ref):
    lanes = pl.ds(pl.multiple_of(ts * LANES, LANES), LANES)

    def scores(c):
        hc = 2 * h + c
        qq = qt_ref[pl.ds(pl.multiple_of(hc * N_KEYS, N_KEYS), N_KEYS), lanes]
        return _mm(keys_ref[hc], qq)

    return scores(0), scores(1)


def _peer_select(h, ts, keys_ref, qt_ref, a_ref, thr_ref, e2_ref, cmp_ref):
    k = PEER_TOPK
    s1, s2 = _half_scores(h, ts, keys_ref, qt_ref)
    v1, distinct1 = _top_values(s1, k)
    v2, distinct2 = _top_values(s2, k)
    cand = _candidates(v1, v2, k)
    vc, distinct_c = _top_values(cand, k)
    lows = _partner_lows(cand >= vc[k - 1], v2, k)
    thr = jnp.full(s1.shape, jnp.inf, F32)
    for a in range(k):
        thr = jnp.where(s1 == v1[a], lows[a], thr)
    _store_gate_tables(h, ts, a_ref, thr_ref, e2_ref, cmp_ref, s1, s2, v1[0], v2[0], vc, thr, s2)
    return jnp.min(jnp.where(distinct1 & distinct2 & distinct_c, 1.0, 0.0)) > 0.5


def _peer_select_ranked(h, ts, keys_ref, qt_ref, a_ref, thr_ref, e2_ref, cmp_ref):
    k = PEER_TOPK
    s1, s2 = _half_scores(h, ts, keys_ref, qt_ref)
    v1, r1 = _top_rank(s1, k)
    v2, r2 = _top_rank(s2, k)
    cand = _candidates(v1, v2, k)
    row = lax.broadcasted_iota(jnp.int32, cand.shape, 0)
    rb = row - KQ * k
    order = jnp.where(row < KQ * k, row, (rb % k) * k + rb // k).astype(F32)
    vc, rc = _top_rank(cand, k, order)
    counts = _partner_counts(jnp.where(rc < float(k), 1.0, 0.0), k)
    n = jnp.zeros_like(s1)
    for a in range(k):
        n = jnp.where(r1 == float(a), counts[a], n)
    _store_gate_tables(h, ts, a_ref, thr_ref, e2_ref, cmp_ref, s1, s2, v1[0], v2[0], vc, 0.5 - n, -r2)


def _peer_body(x_ref, g_ref, wqt_ref, keys_ref, u_ref, vt_ref, o_ref,
               ht_ref, qt_ref, acc_ref, act0_ref, act1_ref, gate0_ref, gate1_ref, a_ref, thr_ref, e2_ref, cmp_ref):
    s = pl.program_id(1)
    n_blocks = pl.num_programs(1) - 2
    tm = x_ref.shape[0]
    te = u_ref.shape[0]
    nts = tm // LANES

    @pl.when(s == 0)
    def _():
        hn = _rms(x_ref[...], g_ref[...])
        ht = hn.T.astype(BF)
        ht_ref[...] = ht
        qt_ref[...] = _mm(wqt_ref[...], ht).astype(BF)
        acc_ref[...] = jnp.zeros(acc_ref.shape, F32)
        gate1_ref[...] = jnp.zeros(gate1_ref.shape, BF)

        tables = (keys_ref, qt_ref, a_ref, thr_ref, e2_ref, cmp_ref)

        def select(it, carry):
            tiles = [((it * SELECT_UNROLL + u) // nts, (it * SELECT_UNROLL + u) % nts) for u in range(SELECT_UNROLL)]
            no_ties = [_peer_select(h, ts, *tables) for h, ts in tiles]
            for (h, ts), ok in zip(tiles, no_ties):
                @pl.when(jnp.logical_not(ok))
                def _():
                    _peer_select_ranked(h, ts, *tables)
            return carry

        assert (PEER_HEADS * nts) % SELECT_UNROLL == 0
        lax.fori_loop(0, PEER_HEADS * nts // SELECT_UNROLL, select, 0)

    gate_block = jnp.clip(s - 1, 0, n_blocks - 1)

    def stages(act_cur, act_prev, gate_cur, gate_prev, do_act=True, do_gate=True, do_value=True):
        for ii in range(te // N_KEYS):
            lo, hi = ii * N_KEYS, (ii + 1) * N_KEYS
            if do_act and lo % ACT_ROWS == 0:
                rows = slice(lo, lo + ACT_ROWS)
                act_cur[rows, :] = _mm(u_ref[rows, :], ht_ref[...])
            if do_gate:
                i = gate_block * (te // N_KEYS) + ii
                for lt in range(nts):
                    ls = slice(lt * LANES, (lt + 1) * LANES)
                    w = None
                    for h in range(PEER_HEADS):
                        a_row = a_ref[h, lt, pl.ds(i, 1), :]
                        thr_row = thr_ref[h, lt, pl.ds(i, 1), :]
                        t = jnp.where(cmp_ref[h, lt] >= thr_row, e2_ref[h, lt] * a_row, 0.0)
                        w = t if w is None else w + t
                    gate_cur[lo:hi, ls] = (_gelu(act_prev[lo:hi, ls]) * w).astype(BF)
            if do_value and ii in VALUE_AFTER:
                c = VALUE_AFTER.index(ii)
                ks = slice(c * VALUE_DEPTH, (c + 1) * VALUE_DEPTH)
                acc_ref[...] += _mm(vt_ref[:, ks], gate_prev[ks, :])

    full = jnp.logical_and(s > 0, s <= n_blocks)

    @pl.when(s == 0)
    def _():
        stages(act0_ref, None, None, None, do_gate=False, do_value=False)

    @pl.when(jnp.logical_and(full, s % 2 == 0))
    def _():
        stages(act0_ref, act1_ref, gate1_ref, gate0_ref)

    @pl.when(jnp.logical_and(full, s % 2 == 1))
    def _():
        stages(act1_ref, act0_ref, gate0_ref, gate1_ref)

    @pl.when(s == n_blocks + 1)
    def _():
        stages(None, None, None, gate1_ref, do_act=False, do_gate=False)

    @pl.when(s == pl.num_programs(1) - 1)
    def _():
        o_ref[...] = x_ref[...] + acc_ref[...].T


def _peer(x, g, wqt, keys, u, vt):
    n, d = x.shape
    n_exp = u.shape[0]
    tm = _block_rows(n, TM_PEER)
    te = vt.shape[2]
    assert tm % LANES == 0 and n_exp % te == 0 and te % (2 * N_KEYS) == 0
    table = pltpu.VMEM((PEER_HEADS, tm // LANES, N_KEYS, LANES), F32)
    act = pltpu.VMEM((te, tm), F32)
    gate = pltpu.VMEM((te, tm), BF)
    nb = n_exp // te
    assert nb % 2 == 0 and te % ACT_ROWS == 0 and te % VALUE_DEPTH == 0
    return pl.pallas_call(
        _peer_body,
        out_shape=jax.ShapeDtypeStruct((n, d), F32),
        grid=(n // tm, nb + 2),
        in_specs=[pl.BlockSpec((tm, d), lambda i, j: (i, 0)),
                  pl.BlockSpec((1, d), lambda i, j: (0, 0)),
                  pl.BlockSpec(wqt.shape, lambda i, j: (0, 0)),
                  pl.BlockSpec(keys.shape, lambda i, j: (0, 0, 0)),
                  pl.BlockSpec((te, d), lambda i, j: (jnp.minimum(j, nb - 1), 0)),
                  pl.BlockSpec((None, d, te), lambda i, j: (jnp.clip(j - 2, 0, nb - 1), 0, 0))],
        out_specs=pl.BlockSpec((tm, d), lambda i, j: (i, 0)),
        scratch_shapes=[pltpu.VMEM((d, tm), BF), pltpu.VMEM((wqt.shape[0], tm), BF),
                        pltpu.VMEM((d, tm), F32), act, act, gate, gate,
                        table, table, table, table],
        compiler_params=_params(("parallel", "arbitrary")),
        name="peer",
    )(x, g.reshape(1, d), wqt, keys, u, vt)


def kernel(x_prompt, x_sample, cache_k, cache_v, page_table, a_norm_g, a_w_in, a_vnorm_g, a_w_s, a_b_s, a_w_out,
           kv_norm_g, w_k, w_v, k_norm_g, b_norm_g, w_q, q_norm_g, lambda_q1, lambda_k1, lambda_q2, lambda_k2,
           subln_g, w_o, f_norm_g, peer_w_q, peer_keys, peer_u, peer_v):
    bsz, seq, d = x_prompt.shape
    r, dec_seq, _ = x_sample.shape
    assert dec_seq == 1 and seq % CHUNK == 0
    depth = f_norm_g.shape[0]
    n_a = a_norm_g.shape[0]
    nh2 = 2 * N_HEADS
    slopes = tuple(float(v) * LOG2E for v in 2.0 ** (-8.0 * np.arange(1, N_HEADS + 1, dtype=np.float32) / N_HEADS))

    xp = x_prompt.reshape(bsz * seq, d)
    xs = x_sample.reshape(r, d)
    seg = _seg_mean_matrix(nh2 * HEAD_DIM, HEAD_DIM)
    gmlp_v_rows = []
    for l in range(depth):
        if l < n_a:
            win = a_w_in[l].astype(BF)
            wout = a_w_out[l].astype(BF)
            xp = _gmlp_prompt(xp, a_norm_g[l], win, a_vnorm_g[l], a_w_s[l], a_b_s[l], wout)
            xs, vs = _gmlp_sample(xs, a_norm_g[l], win, a_vnorm_g[l], a_w_s[l], a_b_s[l], wout)
            gmlp_v_rows.append(vs.reshape(r, dec_seq, -1))
        else:
            if l == n_a:
                wk = w_k.astype(BF)
                wv = w_v.astype(BF)
                kg = jnp.tile(k_norm_g, nh2).reshape(1, nh2 * HEAD_DIM)
                kp_t, vp, kp_tb, vp_b = _shared_kv(xp, kv_norm_g, wk, wv, seg, kg, batch=bsz)
                ks, vs_new, _, _ = _shared_kv(xs, kv_norm_g, wk, wv, seg, kg)
                k_t = kp_tb.reshape(bsz, nh2, HEAD_DIM, seq)
                v_b = vp_b.reshape(bsz, seq, N_HEADS * V_DIM)
            i = l - n_a
            lam_init = 0.8 - 0.6 * math.exp(-0.3 * l)
            wq = w_q[i].astype(BF)
            qg = jnp.tile(q_norm_g[i], nh2).reshape(1, nh2 * HEAD_DIM)
            lams = [v[i].reshape(1, HEAD_DIM) for v in (lambda_q1, lambda_k1, lambda_q2, lambda_k2)]
            sg = subln_g[i].reshape(1, V_DIM)
            wo = w_o[i].astype(BF)
            q_hm = _q_proj(xp, b_norm_g[i], wq, seg, qg)
            op = _prompt_attention(q_hm, k_t, v_b, lams, sg, lam_init, slopes)
            xp = _o_proj(xp, op.reshape(bsz * seq, -1), wo)
            qs = _q_proj(xs, b_norm_g[i], wq, seg, qg)
            os_ = _decode_attention(qs, ks, vs_new, cache_k, cache_v, page_table, lams, sg, lam_init, slopes)
            xs = _o_proj(xs, os_, wo)
        wqt = peer_w_q[l].T.astype(BF)
        keys = peer_keys[l].reshape(PEER_HEADS * 2, N_KEYS, -1).astype(BF)
        u = peer_u[l].astype(BF)
        vt = peer_v[l].reshape(-1, TE_PEER, d).transpose(0, 2, 1).astype(BF)
        xp = _peer(xp, f_norm_g[l], wqt, keys, u, vt)
        xs = _peer(xs, f_norm_g[l], wqt, keys, u, vt)

    state_gmlp_v = jnp.stack(gmlp_v_rows, axis=0)
    return (xp.reshape(bsz, seq, d), xs.reshape(r, dec_seq, d),
            kp_t.reshape(bsz, nh2, HEAD_DIM, seq).transpose(0, 3, 1, 2), vp.reshape(bsz, seq, N_HEADS, V_DIM),
            ks.reshape(r, dec_seq, nh2, HEAD_DIM), vs_new.reshape(r, dec_seq, N_HEADS, V_DIM),
            state_gmlp_v)
```
